```python
import math
import jax
import jax.numpy as jnp
from jax import lax
import numpy as np

D_MODEL = 2048
BATCH = 8
SEQ = 2048
DEPTH = 2

CTX_LEN = 256
GRID_W = 64
MIX_W = D_MODEL
HALF_W = MIX_W // 2

HY_W = HALF_W
HY_ORDER = 2
HY_SHORT = 3
HY_BANDS = 16
HY_EMB = 1 + 2 * HY_BANDS
HY_FFN = 64
HY_FAST_DECAY = 0.3
HY_SLOW_DECAY = 1.5
HY_DECAY_TARGET = 1e-2

GLA_HEADS = 4
GLA_DK = HALF_W // (2 * GLA_HEADS)
GLA_DV = HALF_W // GLA_HEADS
GLA_RANK = 16
GLA_TAU = 16.0
GLA_CHUNK = 64

ML_HEADS = 4
ML_DH = HALF_W // ML_HEADS
ML_SHORT = 3
ML_CHUNK = 64

DA_HEADS = 8
DA_DH = HALF_W // (2 * DA_HEADS)
DA_DV = 2 * DA_DH
Q_BLOCK = 128
ROPE_BASE = 10000.0

N_EXPERTS = 16
EC_CAPACITY = 2
EXPERT_FF = 2048

NORM_EPS = 1e-6

EV_IN = 3 * HY_W + 2 * GLA_HEADS * GLA_DK + 2 * GLA_HEADS * GLA_DV + 2 * GLA_RANK
ML_IN = 4 * HALF_W + 4 * ML_HEADS
DA_IN = 3 * HALF_W
OD_IN = ML_IN + DA_IN

kernel_name = 'hybrid_hyena_gla_mlstm_diffattn_ecmoe_block'


def rmsnorm(x, g):
    xf = x.astype(jnp.float32)
    y = xf * lax.rsqrt(jnp.mean(xf * xf, axis=-1, keepdims=True) + NORM_EPS)
    return (y * g.astype(jnp.float32)).astype(x.dtype)


def modulate(x, g, shift, scale):
    return rmsnorm(x, g) * (1.0 + scale[:, None]) + shift[:, None]


def split_cols(a, sizes):
    return jnp.split(a, np.cumsum(sizes)[:-1].tolist(), axis=-1)


def short_conv(u, w, b):
    pad = w.shape[0] // 2
    y = lax.conv_general_dilated(u, w[:, None, :].astype(u.dtype), (1,), [(pad, pad)],
                                 dimension_numbers=('NWC', 'WIO', 'NWC'),
                                 feature_group_count=u.shape[-1])
    return y + b.astype(u.dtype)


def to_heads(a, n_heads):
    B, L, _ = a.shape
    return jnp.swapaxes(a.reshape(B, L, n_heads, -1), 1, 2).astype(jnp.float32)


def from_heads(a):
    B, H, L, d = a.shape
    return jnp.swapaxes(a, 1, 2).reshape(B, L, H * d)


def to_chunks(a, size):
    B, H, L = a.shape[:3]
    return jnp.moveaxis(a.reshape((B, H, L // size, size) + a.shape[3:]), 2, 0)


def from_chunks(o):
    o = jnp.moveaxis(o, 0, 2)
    return o.reshape(o.shape[:2] + (-1,) + o.shape[4:])


def flip_seq(a):
    return a[:, :, ::-1]


def hyena_filters(L, w1, b1, freq, w2, b2, w3):
    pos = jnp.arange(L, dtype=jnp.float32)
    t = pos[:, None] / max(L - 1, 1)
    bands = jnp.linspace(1e-4, HY_BANDS - 1, HY_BANDS, dtype=jnp.float32)
    ang = (2.0 * math.pi / L) * pos[:, None] * bands[None, :]
    feats = jnp.concatenate([t, jnp.cos(ang), -jnp.sin(ang)], axis=-1)
    h = jnp.sin(freq[0] * (feats @ w1 + b1))
    h = jnp.sin(freq[1] * (h @ w2 + b2))
    h = (h @ w3).astype(jnp.float32)
    deltas = jnp.abs(jnp.linspace(math.log(HY_DECAY_TARGET) / HY_SLOW_DECAY,
                                  math.log(HY_DECAY_TARGET) / HY_FAST_DECAY, HY_W, dtype=jnp.float32))
    window = jnp.exp(-t * deltas[None, :])
    return h.reshape(L, HY_ORDER, 2, HY_W) * window[:, None, None, :]


def long_conv(u, h_fwd, h_bwd, skip):
    L = u.shape[1]
    filt = jnp.concatenate([h_fwd, jnp.zeros_like(h_fwd[:1]), h_bwd[:0:-1]], axis=0)
    uf = u.astype(jnp.float32)
    spec = jnp.fft.rfft(uf, n=2 * L, axis=1) * jnp.fft.rfft(filt, n=2 * L, axis=0)[None]
    y = jnp.fft.irfft(spec, n=2 * L, axis=1)[:, :L]
    return y + uf * skip.astype(jnp.float32)


def hyena_mix(cols, conv_w, conv_b, filt_params, skip):
    L = cols.shape[1]
    filt = hyena_filters(L, *filt_params)
    x1, x2, v = jnp.split(short_conv(cols, conv_w, conv_b), 3, axis=-1)
    for n, gate in enumerate((x1, x2)):
        v = gate * long_conv(v, filt[:, n, 0], filt[:, n, 1], skip[n])
    return v.astype(cols.dtype)


def gla_scan(q, k, v, logg, s0, emit):
    mask = jnp.tril(jnp.ones((GLA_CHUNK, GLA_CHUNK), dtype=bool))

    def step(S, inp):
        qc, kc, vc, gc = inp
        b = jnp.cumsum(gc, axis=-2)
        b_last = b[..., -1:, :]
        S_new = jnp.exp(b_last[..., 0, :])[..., None] * S + jnp.einsum('bhck,bhcv->bhkv', kc * jnp.exp(b_last - b), vc)
        if not emit:
            return S_new, None
        q_dec = qc * jnp.exp(b)
        att = jnp.where(mask, jnp.einsum('bhik,bhjk->bhij', q_dec, kc * jnp.exp(-b)), 0.0)
        o = jnp.einsum('bhij,bhjv->bhiv', att, vc) + jnp.einsum('bhik,bhkv->bhiv', q_dec, S)
        return S_new, o

    xs = tuple(to_chunks(a, GLA_CHUNK) for a in (q, k, v, logg))
    S_fin, o = lax.scan(step, s0, xs)
    return (from_chunks(o) if emit else None), S_fin


def gla_mix(cols, states, gate_w2, gate_b, norm_g, emit):
    B, L, _ = cols.shape
    q, k, v, r, codes = split_cols(cols, (GLA_HEADS * GLA_DK, GLA_HEADS * GLA_DK, GLA_HEADS * GLA_DV,
                                          GLA_HEADS * GLA_DV, 2 * GLA_RANK))
    qh = to_heads(q, GLA_HEADS) * GLA_DK ** -0.5
    kh = to_heads(k, GLA_HEADS)
    vh = to_heads(v, GLA_HEADS)
    codes = codes.reshape(B, L, 2, GLA_RANK)
    out, finals = None, []
    for d in range(2):
        logg = jax.nn.log_sigmoid((codes[:, :, d] @ gate_w2[d] + gate_b[d]).astype(jnp.float32)) / GLA_TAU
        seqs = (qh, kh, vh, to_heads(logg, GLA_HEADS))
        if d == 1:
            seqs = tuple(flip_seq(a) for a in seqs)
        o, sf = gla_scan(*seqs, states[d], emit)
        finals.append(sf)
        if emit:
            o = flip_seq(o) if d == 1 else o
            out = o if out is None else out + o
    if emit:
        out = (from_heads(rmsnorm(out, norm_g)) * jax.nn.silu(r.astype(jnp.float32))).astype(cols.dtype)
    return out, finals


def mlstm_scan(q, k, v, ig, lf, state, emit):
    mask = jnp.tril(jnp.ones((ML_CHUNK, ML_CHUNK), dtype=bool))

    def step(carry, inp):
        Cm, nv, m = carry
        qc, kc, vc, ic, fc = inp
        b = jnp.cumsum(fc, axis=-1)
        b_last = b[..., -1]
        w_end = b_last[..., None] - b + ic
        m_new = jnp.maximum(b_last + m, jnp.max(w_end, axis=-1))
        keep = jnp.exp(b_last + m - m_new)
        w = jnp.exp(w_end - m_new[..., None])
        C_new = keep[..., None, None] * Cm + jnp.einsum('bhc,bhcv,bhck->bhvk', w, vc, kc)
        n_new = keep[..., None] * nv + jnp.einsum('bhc,bhck->bhk', w, kc)
        if not emit:
            return (C_new, n_new, m_new), None
        a = b + m[..., None]
        dlog = jnp.where(mask, b[..., :, None] - b[..., None, :] + ic[..., None, :], -jnp.inf)
        m_t = jnp.maximum(a, jnp.max(dlog, axis=-1))
        sc = jnp.einsum('bhid,bhjd->bhij', qc, kc) * jnp.exp(dlog - m_t[..., None])
        aw = jnp.exp(a - m_t)
        num = jnp.einsum('bhij,bhjv->bhiv', sc, vc) + aw[..., None] * jnp.einsum('bhvk,bhik->bhiv', Cm, qc)
        den = jnp.sum(sc, axis=-1) + aw * jnp.einsum('bhk,bhik->bhi', nv, qc)
        h = num / jnp.maximum(jnp.abs(den), jnp.exp(-m_t))[..., None]
        return (C_new, n_new, m_new), h

    xs = tuple(to_chunks(a, ML_CHUNK) for a in (q, k, v, ig, lf))
    final, hs = lax.scan(step, state, xs)
    return (from_chunks(hs) if emit else None), final


def mlstm_mix(cols, conv_w, conv_b, igate_b, fgate_b, norm_g, states, emit):
    B, L, _ = cols.shape
    q, k, v, o, g = split_cols(cols, (HALF_W, HALF_W, HALF_W, HALF_W, 4 * ML_HEADS))
    q, k = jnp.split(jax.nn.silu(short_conv(jnp.concatenate([q, k], axis=-1), conv_w, conv_b)), 2, axis=-1)
    qh = to_heads(q, ML_HEADS)
    kh = to_heads(k, ML_HEADS) * ML_DH ** -0.5
    vh = to_heads(v, ML_HEADS)
    g = jnp.transpose(g.reshape(B, L, 2, 2, ML_HEADS).astype(jnp.float32), (0, 4, 2, 3, 1))
    hsum, finals = None, []
    for d in range(2):
        ig = g[:, :, d, 0] + igate_b[d].astype(jnp.float32)[None, :, None]
        lf = jax.nn.log_sigmoid(g[:, :, d, 1] + fgate_b[d].astype(jnp.float32)[None, :, None])
        seqs = (qh, kh, vh, ig, lf)
        if d == 1:
            seqs = tuple(flip_seq(a) for a in seqs)
        h, st = mlstm_scan(*seqs, states[d], emit)
        finals.append(st)
        if emit:
            h = flip_seq(h) if d == 1 else h
            hsum = h if hsum is None else hsum + h
    out = None
    if emit:
        out = (from_heads(rmsnorm(hsum, norm_g)) * jax.nn.sigmoid(o.astype(jnp.float32))).astype(cols.dtype)
    return out, finals


def axial_rope(x):
    n = x.shape[1]
    rows_n = n // GRID_W
    row = jnp.repeat(jnp.arange(rows_n), GRID_W).astype(jnp.float32)
    col = jnp.tile(jnp.arange(GRID_W), rows_n).astype(jnp.float32)
    nf = DA_DH // 4
    inv = ROPE_BASE ** (-jnp.arange(nf, dtype=jnp.float32) / nf)
    ang = jnp.stack([row[:, None] * inv, col[:, None] * inv], axis=1)
    cos = jnp.cos(ang)[None, :, None, None]
    sin = jnp.sin(ang)[None, :, None, None]
    xr = x.astype(jnp.float32).reshape(x.shape[:-1] + (2, 2, nf))
    xa, xb = xr[..., 0, :], xr[..., 1, :]
    out = jnp.stack([xa * cos - xb * sin, xb * cos + xa * sin], axis=-2)
    return out.reshape(x.shape).astype(x.dtype)


def diff_attention(cols_ctx, cols_lat, lam_vecs, norm_g, lam_init, emit_ctx):
    def split_qkv(cols):
        B, L, _ = cols.shape
        q, k, v = jnp.split(cols, 3, axis=-1)
        return (q.reshape(B, L, DA_HEADS, 2, DA_DH), k.reshape(B, L, DA_HEADS, 2, DA_DH),
                v.reshape(B, L, DA_HEADS, DA_DV))

    q_c, k_c, v_c = split_qkv(cols_ctx)
    q_l, k_l, v_l = split_qkv(cols_lat)
    q_l, k_l = axial_rope(q_l), axial_rope(k_l)
    lv = lam_vecs.astype(jnp.float32)
    lam = jnp.exp(jnp.sum(lv[0] * lv[1])) - jnp.exp(jnp.sum(lv[2] * lv[3])) + lam_init
    scale = DA_DH ** -0.5

    def attend(qb, keys, vals):
        s = jnp.einsum('bqhcd,bkhcd->bhcqk', qb, keys).astype(jnp.float32) * scale
        p = jax.nn.softmax(s, axis=-1)
        a = p[:, :, 0] - lam * p[:, :, 1]
        o = jnp.einsum('bhqk,bkhv->bqhv', a.astype(vals.dtype), vals)
        return rmsnorm(o, norm_g) * (1.0 - lam_init)

    B, n = q_l.shape[:2]
    keys = jnp.concatenate([k_c, k_l], axis=1)
    vals = jnp.concatenate([v_c, v_l], axis=1)
    qb = jnp.moveaxis(q_l.reshape((B, n // Q_BLOCK, Q_BLOCK) + q_l.shape[2:]), 1, 0)
    o_l = lax.map(lambda blk: attend(blk, keys, vals), qb)
    o_l = jnp.moveaxis(o_l, 0, 1).reshape(B, n, DA_HEADS * DA_DV)
    o_c = None
    if emit_ctx:
        o_c = attend(q_c, k_c, v_c).reshape(B, -1, DA_HEADS * DA_DV)
    return o_c, o_l


def expert_choice_ffn(h, router_w, w1, w3, w2):
    B, T, _ = h.shape
    cap = EC_CAPACITY * T // N_EXPERTS
    aff = jax.nn.softmax((h @ router_w).astype(jnp.float32), axis=-1)
    g, idx = lax.top_k(jnp.swapaxes(aff, 1, 2), cap)
    bidx = jnp.arange(B)[:, None, None]
    xs = h[bidx, idx]
    hid = jax.nn.silu(jnp.einsum('becd,edf->becf', xs, w1)) * jnp.einsum('becd,edf->becf', xs, w3)
    ys = jnp.einsum('becf,efd->becd', hid, w2) * g[..., None].astype(h.dtype)
    return jnp.zeros_like(h).at[bidx, idx].add(ys)


def even_mixer(h_ctx, h_lat, in_w, hy_conv_w, hy_conv_b, hy_pos_w1, hy_pos_b1, hy_sin_freq, hy_pos_w2,
               hy_pos_b2, hy_pos_w3, hy_bias, gla_gate_w2, gla_gate_b, gla_norm_g, emit_ctx):
    p_ctx, p_lat = h_ctx @ in_w, h_lat @ in_w
    hy_n = 3 * HY_W
    filt_params = (hy_pos_w1, hy_pos_b1, hy_sin_freq, hy_pos_w2, hy_pos_b2, hy_pos_w3)
    hy_lat = hyena_mix(p_lat[..., :hy_n], hy_conv_w, hy_conv_b, filt_params, hy_bias)
    B = p_ctx.shape[0]
    zero = jnp.zeros((B, GLA_HEADS, GLA_DK, GLA_DV), jnp.float32)
    gla_ctx, st = gla_mix(p_ctx[..., hy_n:], (zero, zero), gla_gate_w2, gla_gate_b, gla_norm_g, emit_ctx)
    gla_lat, _ = gla_mix(p_lat[..., hy_n:], st, gla_gate_w2, gla_gate_b, gla_norm_g, True)
    y_lat = jnp.concatenate([hy_lat, gla_lat], axis=-1)
    y_ctx = None
    if emit_ctx:
        hy_ctx = hyena_mix(p_ctx[..., :hy_n], hy_conv_w, hy_conv_b, filt_params, hy_bias)
        y_ctx = jnp.concatenate([hy_ctx, gla_ctx], axis=-1)
    return y_ctx, y_lat


def odd_mixer(h_ctx, h_lat, in_w, ml_conv_w, ml_conv_b, ml_igate_b, ml_fgate_b, ml_norm_g,
              da_lambda, da_norm_g, lam_init, emit_ctx):
    p_ctx, p_lat = h_ctx @ in_w, h_lat @ in_w
    B = p_ctx.shape[0]
    zero = (jnp.zeros((B, ML_HEADS, ML_DH, ML_DH), jnp.float32),
            jnp.zeros((B, ML_HEADS, ML_DH), jnp.float32),
            jnp.zeros((B, ML_HEADS), jnp.float32))
    ml_args = (ml_conv_w, ml_conv_b, ml_igate_b, ml_fgate_b, ml_norm_g)
    ml_ctx, st = mlstm_mix(p_ctx[..., :ML_IN], *ml_args, (zero, zero), emit_ctx)
    ml_lat, _ = mlstm_mix(p_lat[..., :ML_IN], *ml_args, st, True)
    da_ctx, da_lat = diff_attention(p_ctx[..., ML_IN:], p_lat[..., ML_IN:], da_lambda, da_norm_g,
                                    lam_init, emit_ctx)
    y_lat = jnp.concatenate([ml_lat, da_lat], axis=-1)
    y_ctx = jnp.concatenate([ml_ctx, da_ctx], axis=-1) if emit_ctx else None
    return y_ctx, y_lat


def setup_inputs(seed: int = 0) -> dict:
    key = jax.random.key(seed)
    keys = iter(jax.random.split(key, 64))

    def nrm(shape, scale=1.0):
        return scale * jax.random.normal(next(keys), shape, jnp.float32)

    D = D_MODEL
    NE = (DEPTH + 1) // 2
    NO = DEPTH // 2
    return {
        'x': nrm((BATCH, SEQ, D)),
        'c': nrm((BATCH, D)),
        'ctx': nrm((BATCH, CTX_LEN, D)),
        'c_ctx': nrm((D,)),
        'ada_w': nrm((DEPTH, D, 6 * D), 0.5 * D ** -0.5),
        'ada_b': nrm((DEPTH, 6 * D), 0.02),
        'norm1_g': 1.0 + nrm((DEPTH, D), 0.02),
        'norm2_g': 1.0 + nrm((DEPTH, D), 0.02),
        'out_w': nrm((DEPTH, MIX_W, D), MIX_W ** -0.5),
        'router_w': nrm((DEPTH, D, N_EXPERTS), D ** -0.5),
        'moe_w1': nrm((DEPTH, N_EXPERTS, D, EXPERT_FF), D ** -0.5),
        'moe_w3': nrm((DEPTH, N_EXPERTS, D, EXPERT_FF), D ** -0.5),
        'moe_w2': nrm((DEPTH, N_EXPERTS, EXPERT_FF, D), EXPERT_FF ** -0.5),
        'final_g': 1.0 + nrm((D,), 0.02),
        'ev_in_w': nrm((NE, D, EV_IN), D ** -0.5),
        'hy_conv_w': nrm((NE, HY_SHORT, 3 * HY_W), HY_SHORT ** -0.5),
        'hy_conv_b': nrm((NE, 3 * HY_W), 0.02),
        'hy_pos_w1': nrm((NE, HY_EMB, HY_FFN), HY_EMB ** -0.5),
        'hy_pos_b1': nrm((NE, HY_FFN), 0.02),
        'hy_sin_freq': 1.0 + nrm((NE, 2, HY_FFN), 0.1),
        'hy_pos_w2': nrm((NE, HY_FFN, HY_FFN), HY_FFN ** -0.5),
        'hy_pos_b2': nrm((NE, HY_FFN), 0.02),
        'hy_pos_w3': nrm((NE, HY_FFN, HY_ORDER * 2 * HY_W), 0.05 * HY_FFN ** -0.5),
        'hy_bias': nrm((NE, HY_ORDER, HY_W), 0.1),
        'gla_gate_w2': nrm((NE, 2, GLA_RANK, GLA_HEADS * GLA_DK), GLA_RANK ** -0.5),
        'gla_gate_b': nrm((NE, 2, GLA_HEADS * GLA_DK), 0.1),
        'gla_norm_g': 1.0 + nrm((NE, GLA_DV), 0.02),
        'od_in_w': nrm((NO, D, OD_IN), D ** -0.5),
        'ml_conv_w': nrm((NO, ML_SHORT, 2 * HALF_W), ML_SHORT ** -0.5),
        'ml_conv_b': nrm((NO, 2 * HALF_W), 0.02),
        'ml_igate_b': nrm((NO, 2, ML_HEADS), 0.1),
        'ml_fgate_b': jnp.linspace(3.0, 6.0, ML_HEADS, dtype=jnp.float32) + nrm((NO, 2, ML_HEADS), 0.1),
        'ml_norm_g': 1.0 + nrm((NO, ML_DH), 0.02),
        'da_lambda': nrm((NO, 4, DA_DH), 0.1),
        'da_norm_g': 1.0 + nrm((NO, DA_DV), 0.02),
    }


def reference(x, c, ctx, c_ctx, ada_w, ada_b, norm1_g, norm2_g, out_w, router_w, moe_w1, moe_w3, moe_w2,
              final_g, ev_in_w, hy_conv_w, hy_conv_b, hy_pos_w1, hy_pos_b1, hy_sin_freq, hy_pos_w2, hy_pos_b2,
              hy_pos_w3, hy_bias, gla_gate_w2, gla_gate_b, gla_norm_g, od_in_w, ml_conv_w, ml_conv_b,
              ml_igate_b, ml_fgate_b, ml_norm_g, da_lambda, da_norm_g):
    cond_lat = jax.nn.silu(c)
    cond_ctx = jax.nn.silu(c_ctx)[None]
    h_x, h_c = x, ctx
    for l in range(DEPTH):
        last = l == DEPTH - 1
        m_lat = jnp.split(cond_lat @ ada_w[l] + ada_b[l], 6, axis=-1)
        m_ctx = jnp.split(cond_ctx @ ada_w[l] + ada_b[l], 6, axis=-1)
        a_lat = modulate(h_x, norm1_g[l], m_lat[0], m_lat[1])
        a_ctx = modulate(h_c, norm1_g[l], m_ctx[0], m_ctx[1])
        if l % 2 == 0:
            e = l // 2
            y_ctx, y_lat = even_mixer(a_ctx, a_lat, ev_in_w[e], hy_conv_w[e], hy_conv_b[e], hy_pos_w1[e],
                                      hy_pos_b1[e], hy_sin_freq[e], hy_pos_w2[e], hy_pos_b2[e], hy_pos_w3[e],
                                      hy_bias[e], gla_gate_w2[e], gla_gate_b[e], gla_norm_g[e], not last)
        else:
            o = l // 2
            lam_init = 0.8 - 0.6 * math.exp(-0.3 * l)
            y_ctx, y_lat = odd_mixer(a_ctx, a_lat, od_in_w[o], ml_conv_w[o], ml_conv_b[o], ml_igate_b[o],
                                     ml_fgate_b[o], ml_norm_g[o], da_lambda[o], da_norm_g[o], lam_init, not last)
        h_x = h_x + m_lat[2][:, None] * (y_lat @ out_w[l])
        h_x = h_x + m_lat[5][:, None] * expert_choice_ffn(modulate(h_x, norm2_g[l], m_lat[3], m_lat[4]),
                                                          router_w[l], moe_w1[l], moe_w3[l], moe_w2[l])
        if not last:
            h_c = h_c + m_ctx[2][:, None] * (y_ctx @ out_w[l])
            h_c = h_c + m_ctx[5][:, None] * expert_choice_ffn(modulate(h_c, norm2_g[l], m_ctx[3], m_ctx[4]),
                                                              router_w[l], moe_w1[l], moe_w3[l], moe_w2[l])
    return rmsnorm(h_x, final_g)
```

```python
import functools
import math

import jax
import jax.numpy as jnp
import numpy as np
from jax import lax
from jax.experimental import pallas as pl
from jax.experimental.pallas import tpu as pltpu

F32 = jnp.float32
BF16 = jnp.bfloat16
HI = lax.Precision.HIGHEST
NORM_EPS = 1e-6
V7X_VMEM_LIMIT_BYTES = 56 * 1024 * 1024
N_MOD_ROWS = 16

D_MODEL = 2048
HALF_W = D_MODEL // 2
GRID_W = 64
HY_W = HALF_W
HY_ORDER = 2
HY_BANDS = 16
HY_FAST_DECAY = 0.3
HY_SLOW_DECAY = 1.5
HY_DECAY_TARGET = 1e-2
GLA_HEADS = 4
GLA_DK = HALF_W // (2 * GLA_HEADS)
GLA_DV = HALF_W // GLA_HEADS
GLA_RANK = 16
GLA_TAU = 16.0
GLA_CHUNK = 64
ML_HEADS = 4
ML_DH = HALF_W // ML_HEADS
ML_CHUNK = 64
DA_HEADS = 8
DA_DH = HALF_W // (2 * DA_HEADS)
DA_DV = 2 * DA_DH
Q_BLOCK = 128
ROPE_BASE = 10000.0
N_EXPERTS = 16
EC_CAPACITY = 2
ML_IN = 4 * HALF_W + 4 * ML_HEADS


def _cparams(*sem, vmem_mib=None):
    limit = V7X_VMEM_LIMIT_BYTES if vmem_mib is None else vmem_mib * 1024 * 1024
    return pltpu.CompilerParams(dimension_semantics=sem, vmem_limit_bytes=limit)


def _dot(a, b):
    return jnp.dot(a, b, preferred_element_type=F32)


def _ada_kernel(c_ref, w_ref, b_ref, o_ref):
    c = c_ref[...]
    a = c * jax.nn.sigmoid(c)
    a_hi = a.astype(BF16)
    a_lo = (a - a_hi.astype(F32)).astype(BF16)
    w = w_ref[...]
    w_hi = w.astype(BF16)
    w_lo = (w - w_hi.astype(F32)).astype(BF16)
    n = a.shape[0]
    r1 = _dot(jnp.concatenate([a_hi, a_lo], axis=0), w_hi)
    r2 = _dot(a_hi, w_lo)
    o_ref[...] = r1[:n] + r1[n:] + r2 + b_ref[...]


def ada_mod(cvec, ada_w, ada_b, tn=512):
    n_lyr, d, n = ada_w.shape
    r = cvec.shape[0]
    return pl.pallas_call(
        _ada_kernel,
        grid=(n_lyr, n // tn),
        in_specs=[pl.BlockSpec((r, d), lambda l, j: (0, 0)),
                  pl.BlockSpec((None, d, tn), lambda l, j: (l, 0, j)),
                  pl.BlockSpec((None, 1, tn), lambda l, j: (l, 0, j))],
        out_specs=pl.BlockSpec((None, r, tn), lambda l, j: (l, 0, j)),
        out_shape=jax.ShapeDtypeStruct((n_lyr, r, n), F32),
        compiler_params=_cparams("parallel", "parallel"),
        name="ada_mod",
    )(cvec, ada_w, ada_b.reshape(n_lyr, 1, n))


def _mod_spec(d, chunk, row_fn, ngrid):
    if ngrid == 2:
        return pl.BlockSpec((None, 1, d), lambda i, j: (row_fn(i), 0, chunk))
    return pl.BlockSpec((None, 1, d), lambda i: (row_fn(i), 0, chunk))


def _norm_mod(x, g, shift, scale):
    y = x * lax.rsqrt(jnp.mean(x * x, axis=-1, keepdims=True) + NORM_EPS) * g
    return y * (1.0 + scale) + shift


def _inproj_kernel(h_ref, g_ref, sh_ref, sc_ref, w_ref, o_ref, a_scr):
    @pl.when(pl.program_id(1) == 0)
    def _():
        a_scr[...] = _norm_mod(h_ref[...], g_ref[...], sh_ref[...], sc_ref[...]).astype(BF16)

    o_ref[...] = _dot(a_scr[...], w_ref[...])


def in_proj(h, g, mod, row_fn, w, tm=1024, tn=512):
    t, d = h.shape
    n = w.shape[1]
    tm = min(tm, t)
    if n % tn:
        tn = 256 if n % 256 == 0 else 128
    return pl.pallas_call(
        _inproj_kernel,
        grid=(t // tm, n // tn),
        in_specs=[pl.BlockSpec((tm, d), lambda i, j: (i, 0)),
                  pl.BlockSpec((1, d), lambda i, j: (0, 0)),
                  _mod_spec(d, 0, row_fn, 2), _mod_spec(d, 1, row_fn, 2),
                  pl.BlockSpec((d, tn), lambda i, j: (0, j))],
        out_specs=pl.BlockSpec((tm, tn), lambda i, j: (i, j)),
        out_shape=jax.ShapeDtypeStruct((t, n), F32),
        scratch_shapes=[pltpu.VMEM((tm, d), BF16)],
        compiler_params=_cparams("parallel", "arbitrary", vmem_mib=32),
        name="in_proj",
    )(h, g, mod, mod, w)


def _outproj_kernel(ya_ref, yb_ref, w_ref, h_ref, gate_ref, o_ref):
    ka = ya_ref.shape[1]
    acc = _dot(ya_ref[...], w_ref[:ka, :]) + _dot(yb_ref[...], w_ref[ka:, :])
    o_ref[...] = h_ref[...] + gate_ref[...] * acc


def out_proj(ya, yb, w, h, mod, row_fn, tm=1024, tn=512):
    t, d = h.shape
    ka, kb = ya.shape[1], yb.shape[1]
    tm, tn = min(tm, t), min(tn, d)
    return pl.pallas_call(
        _outproj_kernel,
        grid=(t // tm, d // tn),
        in_specs=[pl.BlockSpec((tm, ka), lambda i, j: (i, 0)),
                  pl.BlockSpec((tm, kb), lambda i, j: (i, 0)),
                  pl.BlockSpec((ka + kb, tn), lambda i, j: (0, j)),
                  pl.BlockSpec((tm, tn), lambda i, j: (i, j)),
                  pl.BlockSpec((None, 1, tn), lambda i, j: (row_fn(i), 0, 2 * (d // tn) + j))],
        out_specs=pl.BlockSpec((tm, tn), lambda i, j: (i, j)),
        out_shape=jax.ShapeDtypeStruct((t, d), F32),
        compiler_params=_cparams("parallel", "parallel", vmem_mib=32),
        name="out_proj",
    )(ya, yb, w, h, mod)


def _moeprep_kernel(h_ref, g_ref, sh_ref, sc_ref, rw_ref, a_ref, lg_ref):
    a = _norm_mod(h_ref[...], g_ref[...], sh_ref[...], sc_ref[...])
    a_ref[...] = a.astype(BF16)
    lg_ref[...] = jnp.dot(a, rw_ref[...], precision=HI, preferred_element_type=F32)


def moe_prep(h, g, mod, row_fn, router_w, tm=512):
    t, d = h.shape
    e = router_w.shape[1]
    tm = min(tm, t)
    return pl.pallas_call(
        _moeprep_kernel,
        grid=(t // tm,),
        in_specs=[pl.BlockSpec((tm, d), lambda i: (i, 0)),
                  pl.BlockSpec((1, d), lambda i: (0, 0)),
                  _mod_spec(d, 3, row_fn, 1), _mod_spec(d, 4, row_fn, 1),
                  pl.BlockSpec((d, e), lambda i: (0, 0))],
        out_specs=[pl.BlockSpec((tm, d), lambda i: (i, 0)), pl.BlockSpec((tm, e), lambda i: (i, 0))],
        out_shape=[jax.ShapeDtypeStruct((t, d), BF16), jax.ShapeDtypeStruct((t, e), F32)],
        compiler_params=_cparams("parallel"),
        name="moe_prep",
    )(h, g, mod, mod, router_w)


def _route_kernel(lg_ref, pos_ref, gate_ref, *, cap):
    lg = lg_ref[...]
    ts, ne = lg.shape
    ex = jnp.exp(lg - jnp.max(lg, axis=-1, keepdims=True))
    aff = ex / jnp.sum(ex, axis=-1, keepdims=True)
    bits = pltpu.bitcast(aff, jnp.int32)

    def bisect(i, thr):
        cand = thr | jnp.left_shift(jnp.int32(1), 30 - i)
        cnt = jnp.sum((bits >= cand).astype(F32), axis=0, keepdims=True)
        return jnp.where(cnt >= cap, cand, thr)

    thr = lax.fori_loop(0, 31, bisect, jnp.zeros((1, ne), jnp.int32))
    gt = bits > thr
    eq = bits == thr
    n_gt = jnp.sum(gt.astype(F32), axis=0, keepdims=True)
    r = lax.broadcasted_iota(jnp.int32, (ts, ts), 0)
    c = lax.broadcasted_iota(jnp.int32, (ts, ts), 1)
    tri = (c < r).astype(BF16)
    eq_rank = _dot(tri, eq.astype(BF16))
    sel = gt | (eq & (eq_rank < cap - n_gt))
    pos = _dot(tri, sel.astype(BF16))
    pos_ref[...] = jnp.where(sel, pos, -1.0)
    gate_ref[...] = jnp.where(sel, aff, 0.0)


def moe_route(logits, n_samples, cap):
    t, e = logits.shape
    ts = t // n_samples
    spec = pl.BlockSpec((ts, e), lambda s: (s, 0))
    return pl.pallas_call(
        functools.partial(_route_kernel, cap=cap),
        grid=(n_samples,),
        in_specs=[spec],
        out_specs=[spec, spec],
        out_shape=[jax.ShapeDtypeStruct((t, e), F32)] * 2,
        compiler_params=_cparams("parallel"),
        name="moe_route",
    )(logits)


def _slot_onehot(pos_col, cap):
    slots = lax.broadcasted_iota(jnp.int32, (1, cap), 1).astype(F32)
    return (pos_col == slots).astype(BF16)


def _gather_kernel(pos_ref, a_ref, x_ref, *, cap):
    pos = pos_ref[...]
    a = a_ref[...]
    for e in range(pos.shape[1]):
        pt = _slot_onehot(pos[:, e:e + 1], cap)
        x = lax.dot_general(pt, a, (((0,), (0,)), ((), ())), preferred_element_type=F32)
        x_ref[e] = x.astype(BF16)


def moe_gather(pos, a, n_samples, cap, tn=1024):
    t, d = a.shape
    e = pos.shape[1]
    ts = t // n_samples
    tn = min(tn, d)
    return pl.pallas_call(
        functools.partial(_gather_kernel, cap=cap),
        grid=(n_samples, d // tn),
        in_specs=[pl.BlockSpec((ts, e), lambda s, j: (s, 0)), pl.BlockSpec((ts, tn), lambda s, j: (s, j))],
        out_specs=pl.BlockSpec((e, cap, tn), lambda s, j: (0, s, j)),
        out_shape=jax.ShapeDtypeStruct((e, n_samples * cap, d), BF16),
        compiler_params=_cparams("parallel", "parallel"),
        name="moe_gather",
    )(pos, a)


def _ffn_kernel(*refs, nx, nff):
    x_refs = refs[:nx]
    w1_ref, w3_ref, w2_ref = refs[nx:nx + 3]
    y_refs = refs[nx + 3:2 * nx + 3]
    hid_refs = refs[2 * nx + 3:]
    j = pl.program_id(1)

    @pl.when(j < nff)
    def _():
        w1 = w1_ref[...].astype(BF16)
        w3 = w3_ref[...].astype(BF16)
        for x_ref, hid_ref in zip(x_refs, hid_refs):
            x = x_ref[...]
            h1 = _dot(x, w1)
            h3 = _dot(x, w3)
            hid_ref[j] = (h1 * jax.nn.sigmoid(h1) * h3).astype(BF16)

    @pl.when(j >= nff)
    def _():
        w2 = w2_ref[...].astype(BF16)
        tf = w2.shape[0] // nff
        for y_ref, hid_ref in zip(y_refs, hid_refs):
            acc = _dot(hid_ref[0], w2[:tf])
            for jj in range(1, nff):
                acc += _dot(hid_ref[jj], w2[jj * tf:(jj + 1) * tf])
            y_ref[...] = acc.astype(y_ref.dtype)


def moe_ffn(xs, w1, w3, w2, tf=256, tn=256):
    ne, d, ff = w1.shape
    tf, tn = min(tf, ff), min(tn, d)
    nff, nd = ff // tf, d // tn
    nx = len(xs)
    x_specs = [pl.BlockSpec((None, x.shape[1], d), lambda e, j: (e, 0, 0)) for x in xs]
    up_spec = pl.BlockSpec((None, d, tf), lambda e, j: (e, 0, jnp.minimum(j, nff - 1)))
    down_spec = pl.BlockSpec((None, ff, tn), lambda e, j: (e, 0, jnp.maximum(j - nff, 0)))
    y_specs = [pl.BlockSpec((None, x.shape[1], tn), lambda e, j: (e, 0, jnp.maximum(j - nff, 0))) for x in xs]
    return pl.pallas_call(
        functools.partial(_ffn_kernel, nx=nx, nff=nff),
        grid=(ne, nff + nd),
        in_specs=x_specs + [up_spec, up_spec, down_spec],
        out_specs=y_specs,
        out_shape=[jax.ShapeDtypeStruct(x.shape, BF16) for x in xs],
        scratch_shapes=[pltpu.VMEM((nff, x.shape[1], tf), BF16) for x in xs],
        compiler_params=_cparams("parallel", "arbitrary"),
        name="moe_ffn",
    )(*xs, w1, w3, w2)


def _combine_kernel(pos_ref, gate_ref, y_ref, h_ref, m_ref, o_ref, pt_scr, *, cap):
    ne = pos_ref.shape[1]

    @pl.when(pl.program_id(1) == 0)
    def _():
        pos = pos_ref[...]
        for e in range(ne):
            pt_scr[e] = _slot_onehot(pos[:, e:e + 1], cap)

    gate = gate_ref[...]
    acc = gate[:, 0:1] * _dot(pt_scr[0], y_ref[0])
    for e in range(1, ne):
        acc += gate[:, e:e + 1] * _dot(pt_scr[e], y_ref[e])
    o_ref[...] = h_ref[...] + m_ref[...] * acc


def moe_combine(pos, gate, y, h, mod, row_fn, n_samples, cap, tn=512):
    t, d = h.shape
    e = pos.shape[1]
    ts = t // n_samples
    tn = min(tn, d)
    return pl.pallas_call(
        functools.partial(_combine_kernel, cap=cap),
        grid=(n_samples, d // tn),
        in_specs=[pl.BlockSpec((ts, e), lambda s, j: (s, 0)),
                  pl.BlockSpec((ts, e), lambda s, j: (s, 0)),
                  pl.BlockSpec((e, cap, tn), lambda s, j: (0, s, j)),
                  pl.BlockSpec((ts, tn), lambda s, j: (s, j)),
                  pl.BlockSpec((None, 1, tn), lambda s, j: (row_fn(s), 0, 5 * (d // tn) + j))],
        out_specs=pl.BlockSpec((ts, tn), lambda s, j: (s, j)),
        out_shape=jax.ShapeDtypeStruct((t, d), F32),
        scratch_shapes=[pltpu.VMEM((e, ts, cap), BF16)],
        compiler_params=_cparams("parallel", "arbitrary"),
        name="moe_combine",
    )(pos, gate, y, h, mod)


def _rms_kernel(x_ref, g_ref, o_ref):
    x = x_ref[...]
    o_ref[...] = x * lax.rsqrt(jnp.mean(x * x, axis=-1, keepdims=True) + NORM_EPS) * g_ref[...]


def final_norm(h, g, tm=512):
    t, d = h.shape
    return pl.pallas_call(
        _rms_kernel,
        grid=(t // tm,),
        in_specs=[pl.BlockSpec((tm, d), lambda i: (i, 0)), pl.BlockSpec((1, d), lambda i: (0, 0))],
        out_specs=pl.BlockSpec((tm, d), lambda i: (i, 0)),
        out_shape=jax.ShapeDtypeStruct((t, d), F32),
        compiler_params=_cparams("parallel"),
        name="final_norm",
    )(h, g)


def tile_rows(row_of_sample, ts, tm):
    per = ts // tm
    return lambda i: row_of_sample(i // per)


def moe_layer(streams, g2, mod, router_w, w1, w3, w2):
    routed = []
    for h, row_of_sample, n_samples in streams:
        ts = h.shape[0] // n_samples
        cap = EC_CAPACITY * ts // N_EXPERTS
        tm = min(512, ts)
        a, logits = moe_prep(h, g2, mod, tile_rows(row_of_sample, ts, tm), router_w, tm=tm)
        pos, gate = moe_route(logits, n_samples, cap)
        routed.append((pos, gate, cap, moe_gather(pos, a, n_samples, cap)))
    ys = moe_ffn([r[3] for r in routed], w1, w3, w2)
    return [moe_combine(pos, gate, y, h, mod, row_of_sample, n_samples, cap)
            for (h, row_of_sample, n_samples), (pos, gate, cap, _), y in zip(streams, routed, ys)]


def _dir_tri(n, d):
    i = lax.broadcasted_iota(jnp.int32, (n, n), 0)
    j = lax.broadcasted_iota(jnp.int32, (n, n), 1)
    return (j - i) * (1 - 2 * d) <= 0


def _gla_chunk(q_ref, k_ref, v_ref, c_ref, w2_ref, gb_ref, o_ref, st_ref, d):
    cs = q_ref.shape[0]
    causal = _dir_tri(cs, d)
    logits = jnp.dot(c_ref[...], w2_ref[...], precision=HI, preferred_element_type=F32) + gb_ref[...]
    logg = jax.nn.log_sigmoid(logits) * (1.0 / GLA_TAU)
    b = jnp.dot(causal.astype(F32), logg, precision=HI, preferred_element_type=F32)
    b_last = jnp.sum(logg, axis=0, keepdims=True)
    q = q_ref[...] * (GLA_DK ** -0.5)
    k = k_ref[...]
    q_dec = (q * jnp.exp(b)).astype(BF16)
    k_dec = (k * jnp.exp(-b)).astype(BF16)
    k_end = (k * jnp.exp(b_last - b)).astype(BF16)
    v = v_ref[...].astype(BF16)
    for h in range(GLA_HEADS):
        ks = slice(h * GLA_DK, (h + 1) * GLA_DK)
        vs = slice(h * GLA_DV, (h + 1) * GLA_DV)
        st = st_ref[h]
        att = lax.dot_general(q_dec[:, ks], k_dec[:, ks], (((1,), (1,)), ((), ())), preferred_element_type=F32)
        att = jnp.where(causal, att, 0.0).astype(BF16)
        o = _dot(att, v[:, vs]) + lax.dot_general(q_dec[:, ks], st.astype(BF16), (((1,), (1,)), ((), ())),
                                                  preferred_element_type=F32)
        o_ref[:, vs] = o
        upd = lax.dot_general(v[:, vs], k_end[:, ks], (((0,), (0,)), ((), ())), preferred_element_type=F32)
        st_ref[h] = st * jnp.exp(b_last[:, ks]) + upd


def _gla_kernel(qc_ref, kc_ref, vc_ref, cc_ref, ql_ref, kl_ref, vl_ref, cl_ref, w2_ref, gb_ref,
                oc_ref, ol_ref, st_ref, *, ncc):
    d = pl.program_id(1)
    c = pl.program_id(2)

    @pl.when(c == 0)
    def _():
        st_ref[...] = jnp.zeros_like(st_ref)

    @pl.when(c < ncc)
    def _():
        _gla_chunk(qc_ref, kc_ref, vc_ref, cc_ref, w2_ref, gb_ref, oc_ref, st_ref, d)

    @pl.when(c >= ncc)
    def _():
        _gla_chunk(ql_ref, kl_ref, vl_ref, cl_ref, w2_ref, gb_ref, ol_ref, st_ref, d)


def _scan_chunk_maps(n_samples_chunks_ctx, n_samples_chunks_lat):
    ncc, nlc = n_samples_chunks_ctx, n_samples_chunks_lat

    def ctx_blk(b, d, c):
        i = jnp.minimum(c, ncc - 1)
        return b * ncc + jnp.where(d == 0, i, ncc - 1 - i)

    def lat_blk(b, d, c):
        i = jnp.maximum(c - ncc, 0)
        return b * nlc + jnp.where(d == 0, i, nlc - 1 - i)

    return ctx_blk, lat_blk


def gla_scan_call(p_ctx, p_lat, w2_full, gate_b, n_samples, col0):
    cs = GLA_CHUNK
    hk, hv = GLA_HEADS * GLA_DK, GLA_HEADS * GLA_DV
    ncc, nlc = p_ctx.shape[0] // n_samples // cs, p_lat.shape[0] // n_samples // cs
    ctx_blk, lat_blk = _scan_chunk_maps(ncc, nlc)
    qb, kb, vb = col0 // hk, col0 // hk + 1, (col0 + 2 * hk) // hv
    cb = p_lat.shape[1] // 128 - 1

    def specs(blk):
        return [pl.BlockSpec((cs, hk), lambda b, d, c: (blk(b, d, c), qb)),
                pl.BlockSpec((cs, hk), lambda b, d, c: (blk(b, d, c), kb)),
                pl.BlockSpec((cs, hv), lambda b, d, c: (blk(b, d, c), vb)),
                pl.BlockSpec((cs, 128), lambda b, d, c: (blk(b, d, c), cb))]

    return pl.pallas_call(
        functools.partial(_gla_kernel, ncc=ncc),
        grid=(n_samples, 2, ncc + nlc),
        in_specs=specs(ctx_blk) + specs(lat_blk) + [
            pl.BlockSpec((None, 128, hk), lambda b, d, c: (d, 0, 0)),
            pl.BlockSpec((None, 1, hk), lambda b, d, c: (d, 0, 0))],
        out_specs=[pl.BlockSpec((None, cs, hv), lambda b, d, c: (d, ctx_blk(b, d, c), 0)),
                   pl.BlockSpec((None, cs, hv), lambda b, d, c: (d, lat_blk(b, d, c), 0))],
        out_shape=[jax.ShapeDtypeStruct((2, p_ctx.shape[0], hv), F32),
                   jax.ShapeDtypeStruct((2, p_lat.shape[0], hv), F32)],
        scratch_shapes=[pltpu.VMEM((GLA_HEADS, GLA_DV, GLA_DK), F32)],
        compiler_params=_cparams("parallel", "arbitrary", "arbitrary", vmem_mib=32),
        name="gla_scan",
    )(p_ctx, p_ctx, p_ctx, p_ctx, p_lat, p_lat, p_lat, p_lat, w2_full, gate_b)


def _headnorm_gate_kernel(o_ref, r_ref, g_ref, y_ref, *, n_heads, act):
    o = o_ref[0] + o_ref[1]
    r = r_ref[...]
    gate = r * jax.nn.sigmoid(r) if act == "silu" else jax.nn.sigmoid(r)
    dh = o.shape[1] // n_heads
    for h in range(n_heads):
        s = slice(h * dh, (h + 1) * dh)
        oh = o[:, s]
        yh = oh * lax.rsqrt(jnp.mean(oh * oh, axis=-1, keepdims=True) + NORM_EPS) * g_ref[...]
        y_ref[:, s] = (yh * gate[:, s]).astype(y_ref.dtype)


def headnorm_gate(o2, p, r_col, norm_g, n_heads, act, tm=512):
    _, t, w = o2.shape
    tm = min(tm, t)
    return pl.pallas_call(
        functools.partial(_headnorm_gate_kernel, n_heads=n_heads, act=act),
        grid=(t // tm,),
        in_specs=[pl.BlockSpec((2, tm, w), lambda i: (0, i, 0)),
                  pl.BlockSpec((tm, w), lambda i: (i, r_col // w)),
                  pl.BlockSpec((1, w // n_heads), lambda i: (0, 0))],
        out_specs=pl.BlockSpec((tm, w), lambda i: (i, 0)),
        out_shape=jax.ShapeDtypeStruct((t, w), BF16),
        compiler_params=_cparams("parallel", vmem_mib=32),
        name="headnorm_gate",
    )(o2, p, norm_g.reshape(1, -1))


def gla_mixer(p_ctx, p_lat, gate_w2, gate_b, norm_g, n_samples, col0):
    hk, hv = GLA_HEADS * GLA_DK, GLA_HEADS * GLA_DV
    w2_full = jnp.zeros((2, 128, hk), F32)
    for d in range(2):
        w2_full = w2_full.at[d, d * GLA_RANK:(d + 1) * GLA_RANK].set(gate_w2[d])
    o_ctx, o_lat = gla_scan_call(p_ctx, p_lat, w2_full, gate_b.reshape(2, 1, hk), n_samples, col0)
    r_col = col0 + 2 * hk + hv
    return (headnorm_gate(o_ctx, p_ctx, r_col, norm_g, GLA_HEADS, "silu"),
            headnorm_gate(o_lat, p_lat, r_col, norm_g, GLA_HEADS, "silu"))


def rmsnorm(x, g):
    xf = x.astype(jnp.float32)
    y = xf * lax.rsqrt(jnp.mean(xf * xf, axis=-1, keepdims=True) + NORM_EPS)
    return (y * g.astype(jnp.float32)).astype(x.dtype)


def split_cols(a, sizes):
    return jnp.split(a, np.cumsum(sizes)[:-1].tolist(), axis=-1)


def short_conv(u, w, b):
    pad = w.shape[0] // 2
    y = lax.conv_general_dilated(u, w[:, None, :].astype(u.dtype), (1,), [(pad, pad)],
                                 dimension_numbers=('NWC', 'WIO', 'NWC'),
                                 feature_group_count=u.shape[-1])
    return y + b.astype(u.dtype)


def to_heads(a, n_heads):
    B, L, _ = a.shape
    return jnp.swapaxes(a.reshape(B, L, n_heads, -1), 1, 2).astype(jnp.float32)


def from_heads(a):
    B, H, L, d = a.shape
    return jnp.swapaxes(a, 1, 2).reshape(B, L, H * d)


def to_chunks(a, size):
    B, H, L = a.shape[:3]
    return jnp.moveaxis(a.reshape((B, H, L // size, size) + a.shape[3:]), 2, 0)


def from_chunks(o):
    o = jnp.moveaxis(o, 0, 2)
    return o.reshape(o.shape[:2] + (-1,) + o.shape[4:])


def flip_seq(a):
    return a[:, :, ::-1]


def hyena_filters(L, w1, b1, freq, w2, b2, w3):
    pos = jnp.arange(L, dtype=jnp.float32)
    t = pos[:, None] / max(L - 1, 1)
    bands = jnp.linspace(1e-4, HY_BANDS - 1, HY_BANDS, dtype=jnp.float32)
    ang = (2.0 * math.pi / L) * pos[:, None] * bands[None, :]
    feats = jnp.concatenate([t, jnp.cos(ang), -jnp.sin(ang)], axis=-1)
    h = jnp.sin(freq[0] * (feats @ w1 + b1))
    h = jnp.sin(freq[1] * (h @ w2 + b2))
    h = (h @ w3).astype(jnp.float32)
    deltas = jnp.abs(jnp.linspace(math.log(HY_DECAY_TARGET) / HY_SLOW_DECAY,
                                  math.log(HY_DECAY_TARGET) / HY_FAST_DECAY, HY_W, dtype=jnp.float32))
    window = jnp.exp(-t * deltas[None, :])
    return h.reshape(L, HY_ORDER, 2, HY_W) * window[:, None, None, :]


def long_conv(u, h_fwd, h_bwd, skip):
    L = u.shape[1]
    filt = jnp.concatenate([h_fwd, jnp.zeros_like(h_fwd[:1]), h_bwd[:0:-1]], axis=0)
    uf = u.astype(jnp.float32)
    spec = jnp.fft.rfft(uf, n=2 * L, axis=1) * jnp.fft.rfft(filt, n=2 * L, axis=0)[None]
    y = jnp.fft.irfft(spec, n=2 * L, axis=1)[:, :L]
    return y + uf * skip.astype(jnp.float32)


def hyena_mix(cols, conv_w, conv_b, filt_params, skip):
    L = cols.shape[1]
    filt = hyena_filters(L, *filt_params)
    x1, x2, v = jnp.split(short_conv(cols, conv_w, conv_b), 3, axis=-1)
    for n, gate in enumerate((x1, x2)):
        v = gate * long_conv(v, filt[:, n, 0], filt[:, n, 1], skip[n])
    return v.astype(cols.dtype)


def gla_scan(q, k, v, logg, s0, emit):
    mask = jnp.tril(jnp.ones((GLA_CHUNK, GLA_CHUNK), dtype=bool))

    def step(S, inp):
        qc, kc, vc, gc = inp
        b = jnp.cumsum(gc, axis=-2)
        b_last = b[..., -1:, :]
        S_new = jnp.exp(b_last[..., 0, :])[..., None] * S + jnp.einsum('bhck,bhcv->bhkv', kc * jnp.exp(b_last - b), vc)
        if not emit:
            return S_new, None
        q_dec = qc * jnp.exp(b)
        att = jnp.where(mask, jnp.einsum('bhik,bhjk->bhij', q_dec, kc * jnp.exp(-b)), 0.0)
        o = jnp.einsum('bhij,bhjv->bhiv', att, vc) + jnp.einsum('bhik,bhkv->bhiv', q_dec, S)
        return S_new, o

    xs = tuple(to_chunks(a, GLA_CHUNK) for a in (q, k, v, logg))
    S_fin, o = lax.scan(step, s0, xs)
    return (from_chunks(o) if emit else None), S_fin


def gla_mix(cols, states, gate_w2, gate_b, norm_g, emit):
    B, L, _ = cols.shape
    q, k, v, r, codes = split_cols(cols, (GLA_HEADS * GLA_DK, GLA_HEADS * GLA_DK, GLA_HEADS * GLA_DV,
                                          GLA_HEADS * GLA_DV, 2 * GLA_RANK))
    qh = to_heads(q, GLA_HEADS) * GLA_DK ** -0.5
    kh = to_heads(k, GLA_HEADS)
    vh = to_heads(v, GLA_HEADS)
    codes = codes.reshape(B, L, 2, GLA_RANK)
    out, finals = None, []
    for d in range(2):
        logg = jax.nn.log_sigmoid((codes[:, :, d] @ gate_w2[d] + gate_b[d]).astype(jnp.float32)) / GLA_TAU
        seqs = (qh, kh, vh, to_heads(logg, GLA_HEADS))
        if d == 1:
            seqs = tuple(flip_seq(a) for a in seqs)
        o, sf = gla_scan(*seqs, states[d], emit)
        finals.append(sf)
        if emit:
            o = flip_seq(o) if d == 1 else o
            out = o if out is None else out + o
    if emit:
        out = (from_heads(rmsnorm(out, norm_g)) * jax.nn.silu(r.astype(jnp.float32))).astype(cols.dtype)
    return out, finals


def mlstm_scan(q, k, v, ig, lf, state, emit):
    mask = jnp.tril(jnp.ones((ML_CHUNK, ML_CHUNK), dtype=bool))

    def step(carry, inp):
        Cm, nv, m = carry
        qc, kc, vc, ic, fc = inp
        b = jnp.cumsum(fc, axis=-1)
        b_last = b[..., -1]
        w_end = b_last[..., None] - b + ic
        m_new = jnp.maximum(b_last + m, jnp.max(w_end, axis=-1))
        keep = jnp.exp(b_last + m - m_new)
        w = jnp.exp(w_end - m_new[..., None])
        C_new = keep[..., None, None] * Cm + jnp.einsum('bhc,bhcv,bhck->bhvk', w, vc, kc)
        n_new = keep[..., None] * nv + jnp.einsum('bhc,bhck->bhk', w, kc)
        if not emit:
            return (C_new, n_new, m_new), None
        a = b + m[..., None]
        dlog = jnp.where(mask, b[..., :, None] - b[..., None, :] + ic[..., None, :], -jnp.inf)
        m_t = jnp.maximum(a, jnp.max(dlog, axis=-1))
        sc = jnp.einsum('bhid,bhjd->bhij', qc, kc) * jnp.exp(dlog - m_t[..., None])
        aw = jnp.exp(a - m_t)
        num = jnp.einsum('bhij,bhjv->bhiv', sc, vc) + aw[..., None] * jnp.einsum('bhvk,bhik->bhiv', Cm, qc)
        den = jnp.sum(sc, axis=-1) + aw * jnp.einsum('bhk,bhik->bhi', nv, qc)
        h = num / jnp.maximum(jnp.abs(den), jnp.exp(-m_t))[..., None]
        return (C_new, n_new, m_new), h

    xs = tuple(to_chunks(a, ML_CHUNK) for a in (q, k, v, ig, lf))
    final, hs = lax.scan(step, state, xs)
    return (from_chunks(hs) if emit else None), final


def mlstm_mix(cols, conv_w, conv_b, igate_b, fgate_b, norm_g, states, emit):
    B, L, _ = cols.shape
    q, k, v, o, g = split_cols(cols, (HALF_W, HALF_W, HALF_W, HALF_W, 4 * ML_HEADS))
    q, k = jnp.split(jax.nn.silu(short_conv(jnp.concatenate([q, k], axis=-1), conv_w, conv_b)), 2, axis=-1)
    qh = to_heads(q, ML_HEADS)
    kh = to_heads(k, ML_HEADS) * ML_DH ** -0.5
    vh = to_heads(v, ML_HEADS)
    g = jnp.transpose(g.reshape(B, L, 2, 2, ML_HEADS).astype(jnp.float32), (0, 4, 2, 3, 1))
    hsum, finals = None, []
    for d in range(2):
        ig = g[:, :, d, 0] + igate_b[d].astype(jnp.float32)[None, :, None]
        lf = jax.nn.log_sigmoid(g[:, :, d, 1] + fgate_b[d].astype(jnp.float32)[None, :, None])
        seqs = (qh, kh, vh, ig, lf)
        if d == 1:
            seqs = tuple(flip_seq(a) for a in seqs)
        h, st = mlstm_scan(*seqs, states[d], emit)
        finals.append(st)
        if emit:
            h = flip_seq(h) if d == 1 else h
            hsum = h if hsum is None else hsum + h
    out = None
    if emit:
        out = (from_heads(rmsnorm(hsum, norm_g)) * jax.nn.sigmoid(o.astype(jnp.float32))).astype(cols.dtype)
    return out, finals


def axial_rope(x):
    n = x.shape[1]
    rows_n = n // GRID_W
    row = jnp.repeat(jnp.arange(rows_n), GRID_W).astype(jnp.float32)
    col = jnp.tile(jnp.arange(GRID_W), rows_n).astype(jnp.float32)
    nf = DA_DH // 4
    inv = ROPE_BASE ** (-jnp.arange(nf, dtype=jnp.float32) / nf)
    ang = jnp.stack([row[:, None] * inv, col[:, None] * inv], axis=1)
    cos = jnp.cos(ang)[None, :, None, None]
    sin = jnp.sin(ang)[None, :, None, None]
    xr = x.astype(jnp.float32).reshape(x.shape[:-1] + (2, 2, nf))
    xa, xb = xr[..., 0, :], xr[..., 1, :]
    out = jnp.stack([xa * cos - xb * sin, xb * cos + xa * sin], axis=-2)
    return out.reshape(x.shape).astype(x.dtype)


def diff_attention(cols_ctx, cols_lat, lam_vecs, norm_g, lam_init, emit_ctx):
    def split_qkv(cols):
        B, L, _ = cols.shape
        q, k, v = jnp.split(cols, 3, axis=-1)
        return (q.reshape(B, L, DA_HEADS, 2, DA_DH), k.reshape(B, L, DA_HEADS, 2, DA_DH),
                v.reshape(B, L, DA_HEADS, DA_DV))

    q_c, k_c, v_c = split_qkv(cols_ctx)
    q_l, k_l, v_l = split_qkv(cols_lat)
    q_l, k_l = axial_rope(q_l), axial_rope(k_l)
    lv = lam_vecs.astype(jnp.float32)
    lam = jnp.exp(jnp.sum(lv[0] * lv[1])) - jnp.exp(jnp.sum(lv[2] * lv[3])) + lam_init
    scale = DA_DH ** -0.5

    def attend(qb, keys, vals):
        s = jnp.einsum('bqhcd,bkhcd->bhcqk', qb, keys).astype(jnp.float32) * scale
        p = jax.nn.softmax(s, axis=-1)
        a = p[:, :, 0] - lam * p[:, :, 1]
        o = jnp.einsum('bhqk,bkhv->bqhv', a.astype(vals.dtype), vals)
        return rmsnorm(o, norm_g) * (1.0 - lam_init)

    B, n = q_l.shape[:2]
    keys = jnp.concatenate([k_c, k_l], axis=1)
    vals = jnp.concatenate([v_c, v_l], axis=1)
    qb = jnp.moveaxis(q_l.reshape((B, n // Q_BLOCK, Q_BLOCK) + q_l.shape[2:]), 1, 0)
    o_l = lax.map(lambda blk: attend(blk, keys, vals), qb)
    o_l = jnp.moveaxis(o_l, 0, 1).reshape(B, n, DA_HEADS * DA_DV)
    return None, o_l


def even_mixer(p_ctx, p_lat, hy_conv_w, hy_conv_b, hy_pos_w1, hy_pos_b1, hy_sin_freq, hy_pos_w2,
               hy_pos_b2, hy_pos_w3, hy_bias, gla_gate_w2, gla_gate_b, gla_norm_g, emit_ctx):
    hy_n = 3 * HY_W
    filt_params = (hy_pos_w1, hy_pos_b1, hy_sin_freq, hy_pos_w2, hy_pos_b2, hy_pos_w3)
    hy_lat = hyena_mix(p_lat[..., :hy_n], hy_conv_w, hy_conv_b, filt_params, hy_bias)
    B = p_ctx.shape[0]
    zero = jnp.zeros((B, GLA_HEADS, GLA_DK, GLA_DV), jnp.float32)
    gla_ctx, st = gla_mix(p_ctx[..., hy_n:], (zero, zero), gla_gate_w2, gla_gate_b, gla_norm_g, emit_ctx)
    gla_lat, _ = gla_mix(p_lat[..., hy_n:], st, gla_gate_w2, gla_gate_b, gla_norm_g, True)
    hy_ctx = hyena_mix(p_ctx[..., :hy_n], hy_conv_w, hy_conv_b, filt_params, hy_bias)
    return (hy_ctx, gla_ctx), (hy_lat, gla_lat)


def odd_mixer(p_ctx, p_lat, ml_conv_w, ml_conv_b, ml_igate_b, ml_fgate_b, ml_norm_g,
              da_lambda, da_norm_g, lam_init):
    B = p_ctx.shape[0]
    zero = (jnp.zeros((B, ML_HEADS, ML_DH, ML_DH), jnp.float32),
            jnp.zeros((B, ML_HEADS, ML_DH), jnp.float32),
            jnp.zeros((B, ML_HEADS), jnp.float32))
    ml_args = (ml_conv_w, ml_conv_b, ml_igate_b, ml_fgate_b, ml_norm_g)
    _, st = mlstm_mix(p_ctx[..., :ML_IN], *ml_args, (zero, zero), False)
    ml_lat, _ = mlstm_mix(p_lat[..., :ML_IN], *ml_args, st, True)
    _, da_lat = diff_attention(p_ctx[..., ML_IN:], p_lat[..., ML_IN:], da_lambda, da_norm_g, lam_init, False)
    return ml_lat, da_lat


def _pad_cols(w, mult=128):
    n = w.shape[-1]
    return jnp.pad(w, ((0, 0), (0, (-n) % mult)))


def kernel(x, c, ctx, c_ctx, ada_w, ada_b, norm1_g, norm2_g, out_w, router_w, moe_w1, moe_w3, moe_w2, final_g, ev_in_w, hy_conv_w, hy_conv_b, hy_pos_w1, hy_pos_b1, hy_sin_freq, hy_pos_w2, hy_pos_b2, hy_pos_w3, hy_bias, gla_gate_w2, gla_gate_b, gla_norm_g, od_in_w, ml_conv_w, ml_conv_b, ml_igate_b, ml_fgate_b, ml_norm_g, da_lambda, da_norm_g):
    B, L, D = x.shape
    Lc = ctx.shape[1]
    depth = ada_w.shape[0]
    cvec = jnp.zeros((N_MOD_ROWS, D), F32).at[:B].set(c).at[B].set(c_ctx)
    mods = ada_mod(cvec, ada_w, ada_b)
    lat_row = lambda s: s
    ctx_row = lambda s: s * 0 + B
    h_x = x.reshape(B * L, D)
    h_c = ctx.reshape(B * Lc, D)
    tm_lat = min(1024, L)
    for l in range(depth):
        last = l == depth - 1
        mod = mods[l].reshape(N_MOD_ROWS, 1, 6 * D)
        g1 = norm1_g[l].reshape(1, D)
        in_w = ev_in_w[l // 2] if l % 2 == 0 else od_in_w[l // 2]
        n_in = in_w.shape[1]
        w_in = _pad_cols(in_w).astype(BF16)
        w_out = out_w[l].astype(BF16)
        p_lat2 = in_proj(h_x, g1, mod, tile_rows(lat_row, L, tm_lat), w_in, tm=tm_lat)
        p_ctx2 = in_proj(h_c, g1, mod, tile_rows(ctx_row, Lc, Lc), w_in, tm=Lc)
        p_lat = p_lat2[:, :n_in].reshape(B, L, n_in)
        p_ctx = p_ctx2[:, :n_in].reshape(B, Lc, n_in)
        if l % 2 == 0:
            e = l // 2
            hy_n = 3 * HY_W
            filt_params = (hy_pos_w1[e], hy_pos_b1[e], hy_sin_freq[e], hy_pos_w2[e], hy_pos_b2[e], hy_pos_w3[e])
            ya_l = hyena_mix(p_lat[..., :hy_n], hy_conv_w[e], hy_conv_b[e], filt_params, hy_bias[e])
            ya_c = hyena_mix(p_ctx[..., :hy_n], hy_conv_w[e], hy_conv_b[e], filt_params, hy_bias[e])
            yb_c, yb_l = gla_mixer(p_ctx2, p_lat2, gla_gate_w2[e], gla_gate_b[e], gla_norm_g[e], B, hy_n)
        else:
            o = l // 2
            lam_init = 0.8 - 0.6 * math.exp(-0.3 * l)
            ya_l, yb_l = odd_mixer(p_ctx, p_lat, ml_conv_w[o], ml_conv_b[o], ml_igate_b[o], ml_fgate_b[o],
                                   ml_norm_g[o], da_lambda[o], da_norm_g[o], lam_init)
        half = D // 2
        h_x = out_proj(ya_l.reshape(B * L, half).astype(BF16), yb_l.reshape(B * L, half).astype(BF16), w_out, h_x, mod,
                       tile_rows(lat_row, L, tm_lat), tm=tm_lat)
        streams = [(h_x, lat_row, B)]
        if not last:
            h_c = out_proj(ya_c.reshape(B * Lc, half).astype(BF16), yb_c.reshape(B * Lc, half).astype(BF16), w_out, h_c,
                           mod, tile_rows(ctx_row, Lc, Lc), tm=Lc)
            streams.append((h_c, ctx_row, B))
        new = moe_layer(streams, norm2_g[l].reshape(1, D), mod, router_w[l], moe_w1[l], moe_w3[l], moe_w2[l])
        h_x = new[0]
        if not last:
            h_c = new[1]
    return final_norm(h_x, final_g.reshape(1, D)).reshape(B, L, D)
```

```python
import functools
import math

import jax
import jax.numpy as jnp
import numpy as np
from jax import lax
from jax.experimental import pallas as pl
from jax.experimental.pallas import tpu as pltpu

F32 = jnp.float32
BF16 = jnp.bfloat16
HI = lax.Precision.HIGHEST
NORM_EPS = 1e-6
V7X_VMEM_LIMIT_BYTES = 56 * 1024 * 1024
N_MOD_ROWS = 16

D_MODEL = 2048
HALF_W = D_MODEL // 2
GRID_W = 64
HY_W = HALF_W
HY_ORDER = 2
HY_BANDS = 16
HY_FAST_DECAY = 0.3
HY_SLOW_DECAY = 1.5
HY_DECAY_TARGET = 1e-2
GLA_HEADS = 4
GLA_DK = HALF_W // (2 * GLA_HEADS)
GLA_DV = HALF_W // GLA_HEADS
GLA_RANK = 16
GLA_TAU = 16.0
GLA_CHUNK = 64
ML_HEADS = 4
ML_DH = HALF_W // ML_HEADS
ML_CHUNK = 64
DA_HEADS = 8
DA_DH = HALF_W // (2 * DA_HEADS)
DA_DV = 2 * DA_DH
Q_BLOCK = 128
ROPE_BASE = 10000.0
N_EXPERTS = 16
EC_CAPACITY = 2
ML_IN = 4 * HALF_W + 4 * ML_HEADS


def _cparams(*sem, vmem_mib=None):
    limit = V7X_VMEM_LIMIT_BYTES if vmem_mib is None else vmem_mib * 1024 * 1024
    return pltpu.CompilerParams(dimension_semantics=sem, vmem_limit_bytes=limit)


def _dot(a, b):
    return jnp.dot(a, b, preferred_element_type=F32)


def _ada_kernel(c_ref, w_ref, b_ref, o_ref):
    c = c_ref[...]
    a = c * jax.nn.sigmoid(c)
    a_hi = a.astype(BF16)
    a_lo = (a - a_hi.astype(F32)).astype(BF16)
    w = w_ref[...]
    w_hi = w.astype(BF16)
    w_lo = (w - w_hi.astype(F32)).astype(BF16)
    n = a.shape[0]
    r1 = _dot(jnp.concatenate([a_hi, a_lo], axis=0), w_hi)
    r2 = _dot(a_hi, w_lo)
    o_ref[...] = r1[:n] + r1[n:] + r2 + b_ref[...]


def ada_mod(cvec, ada_w, ada_b, tn=512):
    n_lyr, d, n = ada_w.shape
    r = cvec.shape[0]
    return pl.pallas_call(
        _ada_kernel,
        grid=(n_lyr, n // tn),
        in_specs=[pl.BlockSpec((r, d), lambda l, j: (0, 0)),
                  pl.BlockSpec((None, d, tn), lambda l, j: (l, 0, j)),
                  pl.BlockSpec((None, 1, tn), lambda l, j: (l, 0, j))],
        out_specs=pl.BlockSpec((None, r, tn), lambda l, j: (l, 0, j)),
        out_shape=jax.ShapeDtypeStruct((n_lyr, r, n), F32),
        compiler_params=_cparams("parallel", "parallel"),
        name="ada_mod",
    )(cvec, ada_w, ada_b.reshape(n_lyr, 1, n))


def _mod_spec(d, chunk, row_fn, ngrid):
    if ngrid == 2:
        return pl.BlockSpec((None, 1, d), lambda i, j: (row_fn(i), 0, chunk))
    return pl.BlockSpec((None, 1, d), lambda i: (row_fn(i), 0, chunk))


def _norm_mod(x, g, shift, scale):
    y = x * lax.rsqrt(jnp.mean(x * x, axis=-1, keepdims=True) + NORM_EPS) * g
    return y * (1.0 + scale) + shift


def _inproj_kernel(h_ref, g_ref, sh_ref, sc_ref, w_ref, o_ref, a_scr):
    @pl.when(pl.program_id(1) == 0)
    def _():
        a_scr[...] = _norm_mod(h_ref[...], g_ref[...], sh_ref[...], sc_ref[...]).astype(BF16)

    o_ref[...] = _dot(a_scr[...], w_ref[...])


def in_proj(h, g, mod, row_fn, w, tm=1024, tn=512):
    t, d = h.shape
    n = w.shape[1]
    tm = min(tm, t)
    if n % tn:
        tn = 256 if n % 256 == 0 else 128
    return pl.pallas_call(
        _inproj_kernel,
        grid=(t // tm, n // tn),
        in_specs=[pl.BlockSpec((tm, d), lambda i, j: (i, 0)),
                  pl.BlockSpec((1, d), lambda i, j: (0, 0)),
                  _mod_spec(d, 0, row_fn, 2), _mod_spec(d, 1, row_fn, 2),
                  pl.BlockSpec((d, tn), lambda i, j: (0, j))],
        out_specs=pl.BlockSpec((tm, tn), lambda i, j: (i, j)),
        out_shape=jax.ShapeDtypeStruct((t, n), F32),
        scratch_shapes=[pltpu.VMEM((tm, d), BF16)],
        compiler_params=_cparams("parallel", "arbitrary", vmem_mib=48),
        name="in_proj",
    )(h, g, mod, mod, w)


def _outproj_kernel(ya_ref, yb_ref, w_ref, h_ref, gate_ref, o_ref):
    ka = ya_ref.shape[1]
    acc = _dot(ya_ref[...], w_ref[:ka, :]) + _dot(yb_ref[...], w_ref[ka:, :])
    o_ref[...] = h_ref[...] + gate_ref[...] * acc


def out_proj(ya, yb, w, h, mod, row_fn, tm=1024, tn=512):
    t, d = h.shape
    ka, kb = ya.shape[1], yb.shape[1]
    tm, tn = min(tm, t), min(tn, d)
    return pl.pallas_call(
        _outproj_kernel,
        grid=(t // tm, d // tn),
        in_specs=[pl.BlockSpec((tm, ka), lambda i, j: (i, 0)),
                  pl.BlockSpec((tm, kb), lambda i, j: (i, 0)),
                  pl.BlockSpec((ka + kb, tn), lambda i, j: (0, j)),
                  pl.BlockSpec((tm, tn), lambda i, j: (i, j)),
                  pl.BlockSpec((None, 1, tn), lambda i, j: (row_fn(i), 0, 2 * (d // tn) + j))],
        out_specs=pl.BlockSpec((tm, tn), lambda i, j: (i, j)),
        out_shape=jax.ShapeDtypeStruct((t, d), F32),
        compiler_params=_cparams("parallel", "parallel", vmem_mib=32),
        name="out_proj",
    )(ya, yb, w, h, mod)


def _moeprep_kernel(h_ref, g_ref, sh_ref, sc_ref, rw_ref, a_ref, lg_ref):
    a = _norm_mod(h_ref[...], g_ref[...], sh_ref[...], sc_ref[...])
    a_ref[...] = a.astype(BF16)
    lg_ref[...] = jnp.dot(a, rw_ref[...], precision=HI, preferred_element_type=F32)


def moe_prep(h, g, mod, row_fn, router_w, tm=512):
    t, d = h.shape
    e = router_w.shape[1]
    tm = min(tm, t)
    return pl.pallas_call(
        _moeprep_kernel,
        grid=(t // tm,),
        in_specs=[pl.BlockSpec((tm, d), lambda i: (i, 0)),
                  pl.BlockSpec((1, d), lambda i: (0, 0)),
                  _mod_spec(d, 3, row_fn, 1), _mod_spec(d, 4, row_fn, 1),
                  pl.BlockSpec((d, e), lambda i: (0, 0))],
        out_specs=[pl.BlockSpec((tm, d), lambda i: (i, 0)), pl.BlockSpec((tm, e), lambda i: (i, 0))],
        out_shape=[jax.ShapeDtypeStruct((t, d), BF16), jax.ShapeDtypeStruct((t, e), F32)],
        compiler_params=_cparams("parallel"),
        name="moe_prep",
    )(h, g, mod, mod, router_w)


def _route_kernel(lg_ref, pos_ref, gate_ref, *, cap):
    lg = lg_ref[...]
    ts, ne = lg.shape
    ex = jnp.exp(lg - jnp.max(lg, axis=-1, keepdims=True))
    aff = ex / jnp.sum(ex, axis=-1, keepdims=True)
    bits = pltpu.bitcast(aff, jnp.int32)

    def bisect(i, thr):
        cand = thr | jnp.left_shift(jnp.int32(1), 30 - i)
        cnt = jnp.sum((bits >= cand).astype(F32), axis=0, keepdims=True)
        return jnp.where(cnt >= cap, cand, thr)

    thr = lax.fori_loop(0, 31, bisect, jnp.zeros((1, ne), jnp.int32))
    gt = bits > thr
    eq = bits == thr
    n_gt = jnp.sum(gt.astype(F32), axis=0, keepdims=True)
    r = lax.broadcasted_iota(jnp.int32, (ts, ts), 0)
    c = lax.broadcasted_iota(jnp.int32, (ts, ts), 1)
    tri = (c < r).astype(BF16)
    eq_rank = _dot(tri, eq.astype(BF16))
    sel = gt | (eq & (eq_rank < cap - n_gt))
    pos = _dot(tri, sel.astype(BF16))
    pos_ref[...] = jnp.where(sel, pos, -1.0)
    gate_ref[...] = jnp.where(sel, aff, 0.0)


def moe_route(logits, n_samples, cap):
    t, e = logits.shape
    ts = t // n_samples
    spec = pl.BlockSpec((ts, e), lambda s: (s, 0))
    return pl.pallas_call(
        functools.partial(_route_kernel, cap=cap),
        grid=(n_samples,),
        in_specs=[spec],
        out_specs=[spec, spec],
        out_shape=[jax.ShapeDtypeStruct((t, e), F32)] * 2,
        compiler_params=_cparams("parallel"),
        name="moe_route",
    )(logits)


def _slot_onehot(pos_col, cap):
    slots = lax.broadcasted_iota(jnp.int32, (1, cap), 1).astype(F32)
    return (pos_col == slots).astype(BF16)


def _gather_kernel(pos_ref, a_ref, x_ref, *, cap):
    pos = pos_ref[...]
    a = a_ref[...]
    for e in range(pos.shape[1]):
        pt = _slot_onehot(pos[:, e:e + 1], cap)
        x = lax.dot_general(pt, a, (((0,), (0,)), ((), ())), preferred_element_type=F32)
        x_ref[e] = x.astype(BF16)


def moe_gather(pos, a, n_samples, cap, tn=1024):
    t, d = a.shape
    e = pos.shape[1]
    ts = t // n_samples
    tn = min(tn, d)
    return pl.pallas_call(
        functools.partial(_gather_kernel, cap=cap),
        grid=(n_samples, d // tn),
        in_specs=[pl.BlockSpec((ts, e), lambda s, j: (s, 0)), pl.BlockSpec((ts, tn), lambda s, j: (s, j))],
        out_specs=pl.BlockSpec((e, cap, tn), lambda s, j: (0, s, j)),
        out_shape=jax.ShapeDtypeStruct((e, n_samples * cap, d), BF16),
        compiler_params=_cparams("parallel", "parallel"),
        name="moe_gather",
    )(pos, a)


def _ffn_kernel(*refs, nx, nff):
    x_refs = refs[:nx]
    w1_ref, w3_ref, w2_ref = refs[nx:nx + 3]
    y_refs = refs[nx + 3:2 * nx + 3]
    hid_refs = refs[2 * nx + 3:]
    j = pl.program_id(1)

    @pl.when(j < nff)
    def _():
        w1 = w1_ref[...].astype(BF16)
        w3 = w3_ref[...].astype(BF16)
        for x_ref, hid_ref in zip(x_refs, hid_refs):
            x = x_ref[...]
            h1 = _dot(x, w1)
            h3 = _dot(x, w3)
            hid_ref[j] = (h1 * jax.nn.sigmoid(h1) * h3).astype(BF16)

    @pl.when(j >= nff)
    def _():
        w2 = w2_ref[...].astype(BF16)
        tf = w2.shape[0] // nff
        for y_ref, hid_ref in zip(y_refs, hid_refs):
            acc = _dot(hid_ref[0], w2[:tf])
            for jj in range(1, nff):
                acc += _dot(hid_ref[jj], w2[jj * tf:(jj + 1) * tf])
            y_ref[...] = acc.astype(y_ref.dtype)


def moe_ffn(xs, w1, w3, w2, tf=256, tn=256):
    ne, d, ff = w1.shape
    tf, tn = min(tf, ff), min(tn, d)
    nff, nd = ff // tf, d // tn
    nx = len(xs)
    x_specs = [pl.BlockSpec((None, x.shape[1], d), lambda e, j: (e, 0, 0)) for x in xs]
    up_spec = pl.BlockSpec((None, d, tf), lambda e, j: (e, 0, jnp.minimum(j, nff - 1)))
    down_spec = pl.BlockSpec((None, ff, tn), lambda e, j: (e, 0, jnp.maximum(j - nff, 0)))
    y_specs = [pl.BlockSpec((None, x.shape[1], tn), lambda e, j: (e, 0, jnp.maximum(j - nff, 0))) for x in xs]
    return pl.pallas_call(
        functools.partial(_ffn_kernel, nx=nx, nff=nff),
        grid=(ne, nff + nd),
        in_specs=x_specs + [up_spec, up_spec, down_spec],
        out_specs=y_specs,
        out_shape=[jax.ShapeDtypeStruct(x.shape, BF16) for x in xs],
        scratch_shapes=[pltpu.VMEM((nff, x.shape[1], tf), BF16) for x in xs],
        compiler_params=_cparams("parallel", "arbitrary"),
        name="moe_ffn",
    )(*xs, w1, w3, w2)


def _combine_kernel(pos_ref, gate_ref, y_ref, h_ref, m_ref, o_ref, pt_scr, *, cap):
    ne = pos_ref.shape[1]

    @pl.when(pl.program_id(1) == 0)
    def _():
        pos = pos_ref[...]
        for e in range(ne):
            pt_scr[e] = _slot_onehot(pos[:, e:e + 1], cap)

    gate = gate_ref[...]
    acc = gate[:, 0:1] * _dot(pt_scr[0], y_ref[0])
    for e in range(1, ne):
        acc += gate[:, e:e + 1] * _dot(pt_scr[e], y_ref[e])
    o_ref[...] = h_ref[...] + m_ref[...] * acc


def moe_combine(pos, gate, y, h, mod, row_fn, n_samples, cap, tn=512):
    t, d = h.shape
    e = pos.shape[1]
    ts = t // n_samples
    tn = min(tn, d)
    return pl.pallas_call(
        functools.partial(_combine_kernel, cap=cap),
        grid=(n_samples, d // tn),
        in_specs=[pl.BlockSpec((ts, e), lambda s, j: (s, 0)),
                  pl.BlockSpec((ts, e), lambda s, j: (s, 0)),
                  pl.BlockSpec((e, cap, tn), lambda s, j: (0, s, j)),
                  pl.BlockSpec((ts, tn), lambda s, j: (s, j)),
                  pl.BlockSpec((None, 1, tn), lambda s, j: (row_fn(s), 0, 5 * (d // tn) + j))],
        out_specs=pl.BlockSpec((ts, tn), lambda s, j: (s, j)),
        out_shape=jax.ShapeDtypeStruct((t, d), F32),
        scratch_shapes=[pltpu.VMEM((e, ts, cap), BF16)],
        compiler_params=_cparams("parallel", "arbitrary"),
        name="moe_combine",
    )(pos, gate, y, h, mod)


def _rms_kernel(x_ref, g_ref, o_ref):
    x = x_ref[...]
    o_ref[...] = x * lax.rsqrt(jnp.mean(x * x, axis=-1, keepdims=True) + NORM_EPS) * g_ref[...]


def final_norm(h, g, tm=512):
    t, d = h.shape
    return pl.pallas_call(
        _rms_kernel,
        grid=(t // tm,),
        in_specs=[pl.BlockSpec((tm, d), lambda i: (i, 0)), pl.BlockSpec((1, d), lambda i: (0, 0))],
        out_specs=pl.BlockSpec((tm, d), lambda i: (i, 0)),
        out_shape=jax.ShapeDtypeStruct((t, d), F32),
        compiler_params=_cparams("parallel"),
        name="final_norm",
    )(h, g)


def tile_rows(row_of_sample, ts, tm):
    per = ts // tm
    return lambda i: row_of_sample(i // per)


def moe_layer(streams, g2, mod, router_w, w1, w3, w2):
    routed = []
    for h, row_of_sample, n_samples in streams:
        ts = h.shape[0] // n_samples
        cap = EC_CAPACITY * ts // N_EXPERTS
        tm = min(512, ts)
        a, logits = moe_prep(h, g2, mod, tile_rows(row_of_sample, ts, tm), router_w, tm=tm)
        pos, gate = moe_route(logits, n_samples, cap)
        routed.append((pos, gate, cap, moe_gather(pos, a, n_samples, cap)))
    ys = moe_ffn([r[3] for r in routed], w1, w3, w2)
    return [moe_combine(pos, gate, y, h, mod, row_of_sample, n_samples, cap)
            for (h, row_of_sample, n_samples), (pos, gate, cap, _), y in zip(streams, routed, ys)]


def _dir_tri(n, d):
    i = lax.broadcasted_iota(jnp.int32, (n, n), 0)
    j = lax.broadcasted_iota(jnp.int32, (n, n), 1)
    return (j - i) * (1 - 2 * d) <= 0


def _gla_chunk(q_ref, k_ref, v_ref, c_ref, w2_ref, gb_ref, o_ref, st_ref, d):
    cs = q_ref.shape[0]
    causal = _dir_tri(cs, d)
    logits = jnp.dot(c_ref[...], w2_ref[...], precision=HI, preferred_element_type=F32) + gb_ref[...]
    logg = jax.nn.log_sigmoid(logits) * (1.0 / GLA_TAU)
    b = jnp.dot(causal.astype(F32), logg, precision=HI, preferred_element_type=F32)
    b_last = jnp.sum(logg, axis=0, keepdims=True)
    q = q_ref[...] * (GLA_DK ** -0.5)
    k = k_ref[...]
    q_dec = (q * jnp.exp(b)).astype(BF16)
    k_dec = (k * jnp.exp(-b)).astype(BF16)
    k_end = (k * jnp.exp(b_last - b)).astype(BF16)
    v = v_ref[...].astype(BF16)
    for h in range(GLA_HEADS):
        ks = slice(h * GLA_DK, (h + 1) * GLA_DK)
        vs = slice(h * GLA_DV, (h + 1) * GLA_DV)
        st = st_ref[h]
        att = lax.dot_general(q_dec[:, ks], k_dec[:, ks], (((1,), (1,)), ((), ())), preferred_element_type=F32)
        att = jnp.where(causal, att, 0.0).astype(BF16)
        o = _dot(att, v[:, vs]) + lax.dot_general(q_dec[:, ks], st.astype(BF16), (((1,), (1,)), ((), ())),
                                                  preferred_element_type=F32)
        o_ref[:, vs] = o
        upd = lax.dot_general(v[:, vs], k_end[:, ks], (((0,), (0,)), ((), ())), preferred_element_type=F32)
        st_ref[h] = st * jnp.exp(b_last[:, ks]) + upd


def _gla_kernel(qc_ref, kc_ref, vc_ref, cc_ref, ql_ref, kl_ref, vl_ref, cl_ref, w2_ref, gb_ref,
                oc_ref, ol_ref, st_ref, *, ncc):
    d = pl.program_id(1)
    c = pl.program_id(2)

    @pl.when(c == 0)
    def _():
        st_ref[...] = jnp.zeros_like(st_ref)

    @pl.when(c < ncc)
    def _():
        _gla_chunk(qc_ref, kc_ref, vc_ref, cc_ref, w2_ref, gb_ref, oc_ref, st_ref, d)

    @pl.when(c >= ncc)
    def _():
        _gla_chunk(ql_ref, kl_ref, vl_ref, cl_ref, w2_ref, gb_ref, ol_ref, st_ref, d)


def _scan_chunk_maps(n_samples_chunks_ctx, n_samples_chunks_lat):
    ncc, nlc = n_samples_chunks_ctx, n_samples_chunks_lat

    def ctx_blk(b, d, c):
        i = jnp.minimum(c, ncc - 1)
        return b * ncc + jnp.where(d == 0, i, ncc - 1 - i)

    def lat_blk(b, d, c):
        i = jnp.maximum(c - ncc, 0)
        return b * nlc + jnp.where(d == 0, i, nlc - 1 - i)

    return ctx_blk, lat_blk


def gla_scan_call(p_ctx, p_lat, w2_full, gate_b, n_samples, col0, col_codes):
    cs = GLA_CHUNK
    hk, hv = GLA_HEADS * GLA_DK, GLA_HEADS * GLA_DV
    ncc, nlc = p_ctx.shape[0] // n_samples // cs, p_lat.shape[0] // n_samples // cs
    ctx_blk, lat_blk = _scan_chunk_maps(ncc, nlc)
    qb, kb, vb = col0 // hk, col0 // hk + 1, (col0 + 2 * hk) // hv
    cb = col_codes // LANES

    def specs(blk):
        return [pl.BlockSpec((cs, hk), lambda b, d, c: (blk(b, d, c), qb)),
                pl.BlockSpec((cs, hk), lambda b, d, c: (blk(b, d, c), kb)),
                pl.BlockSpec((cs, hv), lambda b, d, c: (blk(b, d, c), vb)),
                pl.BlockSpec((cs, 128), lambda b, d, c: (blk(b, d, c), cb))]

    return pl.pallas_call(
        functools.partial(_gla_kernel, ncc=ncc),
        grid=(n_samples, 2, ncc + nlc),
        in_specs=specs(ctx_blk) + specs(lat_blk) + [
            pl.BlockSpec((None, 128, hk), lambda b, d, c: (d, 0, 0)),
            pl.BlockSpec((None, 1, hk), lambda b, d, c: (d, 0, 0))],
        out_specs=[pl.BlockSpec((None, cs, hv), lambda b, d, c: (d, ctx_blk(b, d, c), 0)),
                   pl.BlockSpec((None, cs, hv), lambda b, d, c: (d, lat_blk(b, d, c), 0))],
        out_shape=[jax.ShapeDtypeStruct((2, p_ctx.shape[0], hv), F32),
                   jax.ShapeDtypeStruct((2, p_lat.shape[0], hv), F32)],
        scratch_shapes=[pltpu.VMEM((GLA_HEADS, GLA_DV, GLA_DK), F32)],
        compiler_params=_cparams("parallel", "arbitrary", "arbitrary", vmem_mib=32),
        name="gla_scan",
    )(p_ctx, p_ctx, p_ctx, p_ctx, p_lat, p_lat, p_lat, p_lat, w2_full, gate_b)


def _headnorm_gate_kernel(o_ref, r_ref, g_ref, y_ref, *, n_heads, act):
    o = o_ref[0] + o_ref[1]
    r = r_ref[...]
    gate = r * jax.nn.sigmoid(r) if act == "silu" else jax.nn.sigmoid(r)
    dh = o.shape[1] // n_heads
    for h in range(n_heads):
        s = slice(h * dh, (h + 1) * dh)
        oh = o[:, s]
        yh = oh * lax.rsqrt(jnp.mean(oh * oh, axis=-1, keepdims=True) + NORM_EPS) * g_ref[...]
        y_ref[:, s] = (yh * gate[:, s]).astype(y_ref.dtype)


def headnorm_gate(o2, p, r_col, norm_g, n_heads, act, tm=512):
    _, t, w = o2.shape
    tm = min(tm, t)
    return pl.pallas_call(
        functools.partial(_headnorm_gate_kernel, n_heads=n_heads, act=act),
        grid=(t // tm,),
        in_specs=[pl.BlockSpec((2, tm, w), lambda i: (0, i, 0)),
                  pl.BlockSpec((tm, w), lambda i: (i, r_col // w)),
                  pl.BlockSpec((1, w // n_heads), lambda i: (0, 0))],
        out_specs=pl.BlockSpec((tm, w), lambda i: (i, 0)),
        out_shape=jax.ShapeDtypeStruct((t, w), BF16),
        compiler_params=_cparams("parallel", vmem_mib=32),
        name="headnorm_gate",
    )(o2, p, norm_g.reshape(1, -1))


def gla_mixer(p_ctx, p_lat, gate_w2, gate_b, norm_g, n_samples, col0):
    hk, hv = GLA_HEADS * GLA_DK, GLA_HEADS * GLA_DV
    w2_full = jnp.zeros((2, 128, hk), F32)
    for d in range(2):
        w2_full = w2_full.at[d, d * GLA_RANK:(d + 1) * GLA_RANK].set(gate_w2[d])
    r_col = col0 + 2 * hk + hv
    o_ctx, o_lat = gla_scan_call(p_ctx, p_lat, w2_full, gate_b.reshape(2, 1, hk), n_samples, col0, r_col + hv)
    return (headnorm_gate(o_ctx, p_ctx, r_col, norm_g, GLA_HEADS, "silu"),
            headnorm_gate(o_lat, p_lat, r_col, norm_g, GLA_HEADS, "silu"))


ROPE_NF = DA_DH // 4
LANES = 128


def _rope_table_kernel(cos_ref, sin_ref):
    n = cos_ref.shape[0]
    t = lax.broadcasted_iota(jnp.int32, (n, LANES), 0)
    j = lax.broadcasted_iota(jnp.int32, (n, LANES), 1)
    dd = j % DA_DH
    is_col = (dd // (2 * ROPE_NF)) == 1
    is_xb = ((dd // ROPE_NF) % 2) == 1
    f = (dd % ROPE_NF).astype(F32)
    inv = jnp.exp(f * (-math.log(ROPE_BASE) / ROPE_NF))
    pos = jnp.where(is_col, t % GRID_W, t // GRID_W).astype(F32)
    ang = pos * inv
    cos_ref[...] = jnp.cos(ang)
    sin_ref[...] = jnp.where(is_xb, 1.0, -1.0) * jnp.sin(ang)


def rope_tables(n):
    out = jax.ShapeDtypeStruct((n, LANES), F32)
    return pl.pallas_call(_rope_table_kernel, out_shape=[out, out], name="rope_tables")()


def _rope(x, cos, sin_signed):
    lane = lax.broadcasted_iota(jnp.int32, x.shape, 1)
    is_xb = ((lane // ROPE_NF) % 2) == 1
    partner = jnp.where(is_xb, pltpu.roll(x, ROPE_NF, 1), pltpu.roll(x, LANES - ROPE_NF, 1))
    return x * cos + partner * sin_signed


def _nt(a, b):
    return lax.dot_general(a, b, (((1,), (1,)), ((), ())), preferred_element_type=F32)


def _diffattn_kernel(q_ref, kl_ref, vl_ref, kc_ref, vc_ref, cq_ref, sq_ref, ck_ref, sk_ref, lam_ref, g_ref, o_ref,
                     *, lam_init):
    lv = lam_ref[...]
    lam = (jnp.exp(jnp.sum(lv[0:1] * lv[1:2], keepdims=True)) - jnp.exp(jnp.sum(lv[2:3] * lv[3:4], keepdims=True))
           + lam_init)
    q = _rope(q_ref[...], cq_ref[...], sq_ref[...]) * (DA_DH ** -0.5)
    kl = _rope(kl_ref[...], ck_ref[...], sk_ref[...]).astype(BF16)
    kc = kc_ref[...].astype(BF16)
    first = lax.broadcasted_iota(jnp.int32, q.shape, 1) < DA_DH
    probs = []
    for comp in range(2):
        qc = jnp.where(first == (comp == 0), q, 0.0).astype(BF16)
        s_l = _nt(qc, kl)
        s_c = _nt(qc, kc)
        m = jnp.maximum(jnp.max(s_l, axis=-1, keepdims=True), jnp.max(s_c, axis=-1, keepdims=True))
        e_l = jnp.exp(s_l - m)
        e_c = jnp.exp(s_c - m)
        inv = 1.0 / (jnp.sum(e_l, axis=-1, keepdims=True) + jnp.sum(e_c, axis=-1, keepdims=True))
        probs.append((e_l * inv, e_c * inv))
    a_l = (probs[0][0] - lam * probs[1][0]).astype(BF16)
    a_c = (probs[0][1] - lam * probs[1][1]).astype(BF16)
    o = _dot(a_l, vl_ref[...].astype(BF16)) + _dot(a_c, vc_ref[...].astype(BF16))
    o = o * lax.rsqrt(jnp.mean(o * o, axis=-1, keepdims=True) + NORM_EPS) * g_ref[...]
    o_ref[...] = (o * (1.0 - lam_init)).astype(o_ref.dtype)


def diff_attention_call(p_ctx, p_lat, n_samples, col_q, lam_vecs, norm_g, lam_init, tq=256):
    lt, lc = p_lat.shape[0] // n_samples, p_ctx.shape[0] // n_samples
    tq = min(tq, lt)
    nq = lt // tq
    w = DA_HEADS * DA_DV
    qb, kb, vb = col_q // LANES, (col_q + w) // LANES, (col_q + 2 * w) // LANES
    cos, sin = rope_tables(lt)
    return pl.pallas_call(
        functools.partial(_diffattn_kernel, lam_init=lam_init),
        grid=(n_samples, DA_HEADS, nq),
        in_specs=[pl.BlockSpec((tq, LANES), lambda b, h, i: (b * nq + i, qb + h)),
                  pl.BlockSpec((lt, LANES), lambda b, h, i: (b, kb + h)),
                  pl.BlockSpec((lt, LANES), lambda b, h, i: (b, vb + h)),
                  pl.BlockSpec((lc, LANES), lambda b, h, i: (b, kb + h)),
                  pl.BlockSpec((lc, LANES), lambda b, h, i: (b, vb + h)),
                  pl.BlockSpec((tq, LANES), lambda b, h, i: (i, 0)),
                  pl.BlockSpec((tq, LANES), lambda b, h, i: (i, 0)),
                  pl.BlockSpec((lt, LANES), lambda b, h, i: (0, 0)),
                  pl.BlockSpec((lt, LANES), lambda b, h, i: (0, 0)),
                  pl.BlockSpec(lam_vecs.shape, lambda b, h, i: (0, 0)),
                  pl.BlockSpec((1, DA_DV), lambda b, h, i: (0, 0))],
        out_specs=pl.BlockSpec((tq, LANES), lambda b, h, i: (b * nq + i, h)),
        out_shape=jax.ShapeDtypeStruct((p_lat.shape[0], w), BF16),
        compiler_params=_cparams("parallel", "parallel", "parallel", vmem_mib=48),
        name="diff_attention",
    )(p_lat, p_lat, p_lat, p_ctx, p_ctx, cos, sin, cos, sin, lam_vecs, norm_g.reshape(1, DA_DV))


def _convsilu_kernel(u_ref, w_ref, b_ref, s_ref, o_ref, *, silu):
    u = u_ref[...]
    n = u.shape[0]
    row = lax.broadcasted_iota(jnp.int32, u.shape, 0)
    prev = jnp.where(row == 0, 0.0, pltpu.roll(u, 1, 0))
    nxt = jnp.where(row == n - 1, 0.0, pltpu.roll(u, n - 1, 0))
    w = w_ref[...]
    y = w[0:1] * prev + w[1:2] * u + w[2:3] * nxt + b_ref[...]
    if silu:
        y = y * jax.nn.sigmoid(y)
    o_ref[...] = y * s_ref[...]


def conv_silu(p, n_samples, width, conv_w, conv_b, col_scale, tn=512, silu=True):
    ls = p.shape[0] // n_samples
    return pl.pallas_call(
        functools.partial(_convsilu_kernel, silu=silu),
        grid=(n_samples, width // tn),
        in_specs=[pl.BlockSpec((ls, tn), lambda s, j: (s, j)),
                  pl.BlockSpec((conv_w.shape[0], tn), lambda s, j: (0, j)),
                  pl.BlockSpec((1, tn), lambda s, j: (0, j)),
                  pl.BlockSpec((1, tn), lambda s, j: (0, j))],
        out_specs=pl.BlockSpec((ls, tn), lambda s, j: (s, j)),
        out_shape=jax.ShapeDtypeStruct((p.shape[0], width), F32),
        compiler_params=_cparams("parallel", "parallel", vmem_mib=32),
        name="conv_silu",
    )(p, conv_w, conv_b.reshape(1, -1), col_scale.reshape(1, -1))


def _mlstm_chunk(q_ref, k_ref, v_ref, g_ref, gb_ref, o_ref, c_ref, n_ref, m_ref, d, emit):
    cs = q_ref.shape[0]
    causal = _dir_tri(cs, d)
    g = g_ref[...] + gb_ref[...]
    lane = lax.broadcasted_iota(jnp.int32, g.shape, 1)
    gl = jnp.where((lane // ML_HEADS) % 2 == 1, jax.nn.log_sigmoid(g), g)
    bcum = jnp.dot(causal.astype(F32), gl, precision=HI, preferred_element_type=F32)
    gl_t = gl.T
    bcum_t = bcum.T
    q = q_ref[...].astype(BF16)
    k = k_ref[...]
    kb = k.astype(BF16)
    v = v_ref[...]

    def pick_col(a, idx):
        return jnp.where(d == 0, a[:, idx:idx + 1], a[:, 2 * ML_HEADS + idx:2 * ML_HEADS + idx + 1])

    def pick_row(a, idx):
        return jnp.where(d == 0, a[idx:idx + 1, :], a[2 * ML_HEADS + idx:2 * ML_HEADS + idx + 1, :])

    for h in range(ML_HEADS):
        hs = slice(h * ML_DH, (h + 1) * ML_DH)
        ic_c, ic_r = pick_col(gl, h), pick_row(gl_t, h)
        fc_c = pick_col(gl, ML_HEADS + h)
        b_c, b_r = pick_col(bcum, ML_HEADS + h), pick_row(bcum_t, ML_HEADS + h)
        b_last = jnp.sum(fc_c, axis=0, keepdims=True)
        m = m_ref[h][:, 0:1]
        cm = c_ref[h]
        nv = n_ref[h]
        w_end_c = b_last - b_c + ic_c
        w_end_r = b_last - b_r + ic_r
        m_new = jnp.maximum(b_last + m, jnp.max(w_end_r, axis=-1, keepdims=True))
        keep = jnp.exp(b_last + m - m_new)
        w_c = jnp.exp(w_end_c - m_new)
        vw = (v[:, hs] * w_c).astype(BF16)
        c_ref[h] = keep * cm + lax.dot_general(vw, kb[:, hs], (((0,), (0,)), ((), ())), preferred_element_type=F32)
        n_ref[h] = keep * nv + jnp.sum(w_c * k[:, hs], axis=0, keepdims=True)
        m_ref[h] = jnp.broadcast_to(m_new, m_ref.shape[1:])
        if emit:
            a_c = b_c + m
            dlog = jnp.where(causal, b_c - b_r + ic_r, -jnp.inf)
            m_t = jnp.maximum(a_c, jnp.max(dlog, axis=-1, keepdims=True))
            sc = _nt(q[:, hs], kb[:, hs]) * jnp.exp(dlog - m_t)
            aw = jnp.exp(a_c - m_t)
            num = _dot(sc.astype(BF16), v[:, hs].astype(BF16)) + aw * _nt(q[:, hs], cm.astype(BF16))
            qf = q_ref[:, hs]
            den = jnp.sum(sc, axis=-1, keepdims=True) + aw * jnp.sum(qf * nv, axis=-1, keepdims=True)
            o_ref[:, hs] = num / jnp.maximum(jnp.abs(den), jnp.exp(-m_t))


def _mlstm_kernel(qc_ref, kc_ref, vc_ref, gc_ref, ql_ref, kl_ref, vl_ref, gl_ref, gb_ref,
                  ol_ref, c_ref, n_ref, m_ref, *, ncc):
    d = pl.program_id(1)
    c = pl.program_id(2)

    @pl.when(c == 0)
    def _():
        c_ref[...] = jnp.zeros_like(c_ref)
        n_ref[...] = jnp.zeros_like(n_ref)
        m_ref[...] = jnp.zeros_like(m_ref)

    @pl.when(c < ncc)
    def _():
        _mlstm_chunk(qc_ref, kc_ref, vc_ref, gc_ref, gb_ref, None, c_ref, n_ref, m_ref, d, False)

    @pl.when(c >= ncc)
    def _():
        _mlstm_chunk(ql_ref, kl_ref, vl_ref, gl_ref, gb_ref, ol_ref, c_ref, n_ref, m_ref, d, True)


def mlstm_scan_call(qk_ctx, qk_lat, p_ctx, p_lat, gate_bias, n_samples, col_v, col_g):
    cs = ML_CHUNK
    w = ML_HEADS * ML_DH
    ncc, nlc = p_ctx.shape[0] // n_samples // cs, p_lat.shape[0] // n_samples // cs
    ctx_blk, lat_blk = _scan_chunk_maps(ncc, nlc)
    vb = col_v // w
    gcol = col_g // LANES

    def specs(blk):
        return [pl.BlockSpec((cs, w), lambda b, d, c: (blk(b, d, c), 0)),
                pl.BlockSpec((cs, w), lambda b, d, c: (blk(b, d, c), 1)),
                pl.BlockSpec((cs, w), lambda b, d, c: (blk(b, d, c), vb)),
                pl.BlockSpec((cs, LANES), lambda b, d, c: (blk(b, d, c), gcol))]

    return pl.pallas_call(
        functools.partial(_mlstm_kernel, ncc=ncc),
        grid=(n_samples, 2, ncc + nlc),
        in_specs=specs(ctx_blk) + specs(lat_blk) + [pl.BlockSpec((1, LANES), lambda b, d, c: (0, 0))],
        out_specs=pl.BlockSpec((None, cs, w), lambda b, d, c: (d, lat_blk(b, d, c), 0)),
        out_shape=jax.ShapeDtypeStruct((2, p_lat.shape[0], w), F32),
        scratch_shapes=[pltpu.VMEM((ML_HEADS, ML_DH, ML_DH), F32), pltpu.VMEM((ML_HEADS, 1, ML_DH), F32),
                        pltpu.VMEM((ML_HEADS, 1, LANES), F32)],
        compiler_params=_cparams("parallel", "arbitrary", "arbitrary", vmem_mib=32),
        name="mlstm_scan",
    )(qk_ctx, qk_ctx, p_ctx, p_ctx, qk_lat, qk_lat, p_lat, p_lat, gate_bias)


def mlstm_mixer(p_ctx, p_lat, conv_w, conv_b, igate_b, fgate_b, norm_g, n_samples, col_g):
    w = ML_HEADS * ML_DH
    scale = jnp.concatenate([jnp.ones((w,), F32), jnp.full((w,), ML_DH ** -0.5, F32)])
    qk_ctx = conv_silu(p_ctx, n_samples, 2 * w, conv_w, conv_b, scale)
    qk_lat = conv_silu(p_lat, n_samples, 2 * w, conv_w, conv_b, scale)
    bias = jnp.concatenate([jnp.stack([igate_b[d], fgate_b[d]]).reshape(-1) for d in range(2)])
    gate_bias = jnp.zeros((1, LANES), F32).at[0, :bias.shape[0]].set(bias)
    o2 = mlstm_scan_call(qk_ctx, qk_lat, p_ctx, p_lat, gate_bias, n_samples, 2 * w, col_g)
    return headnorm_gate(o2, p_lat, 3 * w, norm_g, ML_HEADS, "sigmoid")


HY_FEAT_PAD = 128
DFT_FREQ_PAD = 128


def _hyfilt_kernel(w1_ref, b1_ref, fr_ref, w2_ref, b2_ref, w3_ref, o_ref, *, seq):
    tn = o_ref.shape[1]
    pos = lax.broadcasted_iota(jnp.int32, (seq, HY_FEAT_PAD), 0).astype(F32)
    lane = lax.broadcasted_iota(jnp.int32, (seq, HY_FEAT_PAD), 1)
    t = pos / max(seq - 1, 1)
    band = 1e-4 + ((lane - 1) % HY_BANDS).astype(F32) * ((HY_BANDS - 1 - 1e-4) / (HY_BANDS - 1))
    ang = (2.0 * math.pi / seq) * pos * band
    feats = jnp.where(lane == 0, t, jnp.where(lane <= HY_BANDS, jnp.cos(ang),
                                              jnp.where(lane <= 2 * HY_BANDS, -jnp.sin(ang), 0.0)))
    fr = fr_ref[...]
    h = jnp.sin(fr[0:1] * (jnp.dot(feats, w1_ref[...], precision=HI, preferred_element_type=F32) + b1_ref[...]))
    h = jnp.sin(fr[1:2] * (jnp.dot(h, w2_ref[...], precision=HI, preferred_element_type=F32) + b2_ref[...]))
    h = jnp.dot(h, w3_ref[...], precision=HI, preferred_element_type=F32)
    col = pl.program_id(0) * tn + lax.broadcasted_iota(jnp.int32, (1, tn), 1)
    chan = col % HY_W
    lo, hi = math.log(HY_DECAY_TARGET) / HY_SLOW_DECAY, math.log(HY_DECAY_TARGET) / HY_FAST_DECAY
    delta = jnp.abs(lo + chan.astype(F32) * ((hi - lo) / (HY_W - 1)))
    filt = h * jnp.exp(-t[:, 0:1] * delta)
    is_bwd = (col // HY_W) % 2 == 1
    o_ref[...] = jnp.where(is_bwd & (pos[:, 0:1] == 0.0), 0.0, filt)


def hyena_filters_call(seq, w1, b1, freq, w2, b2, w3, tn=512):
    nf = w1.shape[1]
    n = w3.shape[1]
    w1p = jnp.zeros((HY_FEAT_PAD, nf), F32).at[:w1.shape[0]].set(w1)
    full = lambda shape: pl.BlockSpec(shape, lambda j: (0, 0))
    return pl.pallas_call(
        functools.partial(_hyfilt_kernel, seq=seq),
        grid=(n // tn,),
        in_specs=[full((HY_FEAT_PAD, nf)), full((1, nf)), full((2, nf)), full((nf, nf)), full((1, nf)),
                  pl.BlockSpec((nf, tn), lambda j: (0, j))],
        out_specs=pl.BlockSpec((seq, tn), lambda j: (0, j)),
        out_shape=jax.ShapeDtypeStruct((seq, n), F32),
        compiler_params=_cparams("parallel", vmem_mib=32),
        name="hyena_filters",
    )(w1p, b1.reshape(1, nf), freq, w2, b2.reshape(1, nf), w3)


def _dft_angle(f, s, n):
    k = (f * s) % n
    return k.astype(F32) * (2.0 * math.pi / n)


def _dft_fwd_kernel(o_ref, *, seq, nfreq):
    tr = o_ref.shape[0]
    r = pl.program_id(0) * tr + lax.broadcasted_iota(jnp.int32, (tr, seq), 0)
    s = lax.broadcasted_iota(jnp.int32, (tr, seq), 1)
    f = r % nfreq
    ang = _dft_angle(f, s, 2 * seq)
    val = jnp.where(r >= nfreq, -jnp.sin(ang), jnp.cos(ang))
    o_ref[...] = jnp.where(f <= seq, val, 0.0).astype(o_ref.dtype)


def _dft_inv_kernel(o_ref, *, seq, nfreq):
    tc = o_ref.shape[1]
    c = pl.program_id(0) * tc + lax.broadcasted_iota(jnp.int32, (seq, tc), 1)
    t = lax.broadcasted_iota(jnp.int32, (seq, tc), 0)
    f = c % nfreq
    ang = _dft_angle(f, t, 2 * seq)
    wf = jnp.where((f == 0) | (f == seq), 1.0, jnp.where(f < seq, 2.0, 0.0)) * (0.5 / seq)
    val = jnp.where(c >= nfreq, -jnp.sin(ang), jnp.cos(ang))
    o_ref[...] = (wf * val).astype(o_ref.dtype)


def dft_tables(seq):
    nfreq = seq + DFT_FREQ_PAD
    fwd = pl.pallas_call(
        functools.partial(_dft_fwd_kernel, seq=seq, nfreq=nfreq), grid=(2 * nfreq // 128,),
        out_specs=pl.BlockSpec((128, seq), lambda i: (i, 0)),
        out_shape=jax.ShapeDtypeStruct((2 * nfreq, seq), BF16), compiler_params=_cparams("parallel", vmem_mib=32),
        name="dft_fwd_table")()
    inv = pl.pallas_call(
        functools.partial(_dft_inv_kernel, seq=seq, nfreq=nfreq), grid=(2 * nfreq // 128,),
        out_specs=pl.BlockSpec((seq, 128), lambda j: (0, j)),
        out_shape=jax.ShapeDtypeStruct((seq, 2 * nfreq), BF16), compiler_params=_cparams("parallel", vmem_mib=32),
        name="dft_inv_table")()
    return fwd, inv


def _dft_apply_kernel(a_ref, x_ref, o_ref, xb_scr):
    @pl.when(pl.program_id(2) == 0)
    def _():
        xb_scr[...] = x_ref[...].astype(BF16)

    o_ref[...] = _dot(a_ref[...], xb_scr[...])


def dft_apply(table, x, n_samples, col0, width, tm=256, tn=512):
    m, ls = table.shape
    tn = min(tn, width)
    return pl.pallas_call(
        _dft_apply_kernel,
        grid=(n_samples, width // tn, m // tm),
        in_specs=[pl.BlockSpec((tm, ls), lambda s, j, i: (i, 0)),
                  pl.BlockSpec((ls, tn), lambda s, j, i: (s, col0 // tn + j))],
        out_specs=pl.BlockSpec((None, tm, tn), lambda s, j, i: (s, i, j)),
        out_shape=jax.ShapeDtypeStruct((n_samples, m, width), F32),
        scratch_shapes=[pltpu.VMEM((ls, tn), BF16)],
        compiler_params=_cparams("parallel", "parallel", "arbitrary", vmem_mib=40),
        name="dft_apply",
    )(table, x)


def _conv_back_kernel(a_ref, z_ref, hf_ref, hb_ref, gate_ref, prev_ref, skip_ref, o_ref, y_scr):
    nfreq = z_ref.shape[0] // 2

    @pl.when(pl.program_id(2) == 0)
    def _():
        ure, uim = z_ref[:nfreq], z_ref[nfreq:]
        hre = hf_ref[0, :nfreq] + hb_ref[0, :nfreq]
        him = hf_ref[0, nfreq:] - hb_ref[0, nfreq:]
        y_scr[:nfreq] = (ure * hre - uim * him).astype(BF16)
        y_scr[nfreq:] = (ure * him + uim * hre).astype(BF16)

    y = _dot(a_ref[...], y_scr[...])
    prev = prev_ref[...]
    o_ref[...] = (gate_ref[...] * (y + skip_ref[...] * prev)).astype(o_ref.dtype)


def conv_back(inv_table, z, hspec, order, gate, gate_col, prev, prev_col, skip, out_dtype, tm=512, tn=256):
    ns, m2, c = z.shape
    ls = inv_table.shape[0]
    tm = min(tm, ls)
    hcol = order * 2 * c // tn
    return pl.pallas_call(
        _conv_back_kernel,
        grid=(ns, c // tn, ls // tm),
        in_specs=[pl.BlockSpec((tm, m2), lambda s, j, i: (i, 0)),
                  pl.BlockSpec((None, m2, tn), lambda s, j, i: (s, 0, j)),
                  pl.BlockSpec((1, m2, tn), lambda s, j, i: (0, 0, hcol + j)),
                  pl.BlockSpec((1, m2, tn), lambda s, j, i: (0, 0, hcol + c // tn + j)),
                  pl.BlockSpec((tm, tn), lambda s, j, i: (s * (ls // tm) + i, gate_col // tn + j)),
                  pl.BlockSpec((tm, tn), lambda s, j, i: (s * (ls // tm) + i, prev_col // tn + j)),
                  pl.BlockSpec((1, tn), lambda s, j, i: (0, j))],
        out_specs=pl.BlockSpec((tm, tn), lambda s, j, i: (s * (ls // tm) + i, j)),
        out_shape=jax.ShapeDtypeStruct((ns * ls, c), out_dtype),
        scratch_shapes=[pltpu.VMEM((m2, tn), BF16)],
        compiler_params=_cparams("parallel", "parallel", "arbitrary", vmem_mib=48),
        name="conv_back",
    )(inv_table, z, hspec, hspec, gate, prev, skip)


def hyena_mixer(p, n_samples, conv_w, conv_b, filt_params, skip):
    ls = p.shape[0] // n_samples
    c = HY_W
    fwd, inv = dft_tables(ls)
    filt = hyena_filters_call(ls, *filt_params)
    hspec = dft_apply(fwd, filt, 1, 0, filt.shape[1])
    xc = conv_silu(p, n_samples, 3 * c, conv_w, conv_b, jnp.ones((3 * c,), F32), silu=False)
    v, v_col = xc, 2 * c
    for order in range(HY_ORDER):
        z = dft_apply(fwd, v, n_samples, v_col, c)
        last = order == HY_ORDER - 1
        v = conv_back(inv, z, hspec, order, xc, order * c, v, v_col, skip[order].reshape(1, c),
                      BF16 if last else F32)
        v_col = 0
    return v


def _pad_cols(w, mult=128):
    n = w.shape[-1]
    return jnp.pad(w, ((0, 0), (0, (-n) % mult)))


def kernel(x, c, ctx, c_ctx, ada_w, ada_b, norm1_g, norm2_g, out_w, router_w, moe_w1, moe_w3, moe_w2, final_g, ev_in_w, hy_conv_w, hy_conv_b, hy_pos_w1, hy_pos_b1, hy_sin_freq, hy_pos_w2, hy_pos_b2, hy_pos_w3, hy_bias, gla_gate_w2, gla_gate_b, gla_norm_g, od_in_w, ml_conv_w, ml_conv_b, ml_igate_b, ml_fgate_b, ml_norm_g, da_lambda, da_norm_g):
    B, L, D = x.shape
    Lc = ctx.shape[1]
    depth = ada_w.shape[0]
    cvec = jnp.zeros((N_MOD_ROWS, D), F32).at[:B].set(c).at[B].set(c_ctx)
    mods = ada_mod(cvec, ada_w, ada_b)
    lat_row = lambda s: s
    ctx_row = lambda s: s * 0 + B
    h_x = x.reshape(B * L, D)
    h_c = ctx.reshape(B * Lc, D)
    tm_lat = min(1024, L)
    for l in range(depth):
        last = l == depth - 1
        mod = mods[l].reshape(N_MOD_ROWS, 1, 6 * D)
        g1 = norm1_g[l].reshape(1, D)
        if l % 2 == 0:
            in_w = ev_in_w[l // 2]
        else:
            od = od_in_w[l // 2]
            ml_w = 4 * HALF_W
            in_w = jnp.concatenate([od[:, :ml_w], od[:, ML_IN:], od[:, ml_w:ML_IN]], axis=1)
        w_in = _pad_cols(in_w, 512).astype(BF16)
        w_out = out_w[l].astype(BF16)
        p_lat2 = in_proj(h_x, g1, mod, tile_rows(lat_row, L, tm_lat), w_in, tm=tm_lat)
        p_ctx2 = in_proj(h_c, g1, mod, tile_rows(ctx_row, Lc, Lc), w_in, tm=Lc)
        if l % 2 == 0:
            e = l // 2
            hy_n = 3 * HY_W
            filt_params = (hy_pos_w1[e], hy_pos_b1[e], hy_sin_freq[e], hy_pos_w2[e], hy_pos_b2[e], hy_pos_w3[e])
            ya_l = hyena_mixer(p_lat2, B, hy_conv_w[e], hy_conv_b[e], filt_params, hy_bias[e])
            ya_c = hyena_mixer(p_ctx2, B, hy_conv_w[e], hy_conv_b[e], filt_params, hy_bias[e])
            yb_c, yb_l = gla_mixer(p_ctx2, p_lat2, gla_gate_w2[e], gla_gate_b[e], gla_norm_g[e], B, hy_n)
        else:
            o = l // 2
            lam_init = 0.8 - 0.6 * math.exp(-0.3 * l)
            ya_l = mlstm_mixer(p_ctx2, p_lat2, ml_conv_w[o], ml_conv_b[o], ml_igate_b[o], ml_fgate_b[o], ml_norm_g[o],
                               B, ml_w + 3 * HALF_W)
            yb_l = diff_attention_call(p_ctx2, p_lat2, B, ml_w, da_lambda[o], da_norm_g[o], lam_init)
        h_x = out_proj(ya_l, yb_l, w_out, h_x, mod, tile_rows(lat_row, L, tm_lat), tm=tm_lat)
        streams = [(h_x, lat_row, B)]
        if not last:
            h_c = out_proj(ya_c, yb_c, w_out, h_c, mod, tile_rows(ctx_row, Lc, Lc), tm=Lc)
            streams.append((h_c, ctx_row, B))
        new = moe_layer(streams, norm2_g[l].reshape(1, D), mod, router_w[l], moe_w1[l], moe_w3[l], moe_w2[l])
        h_x = new[0]
        if not last:
            h_c = new[1]
    return final_norm(h_x, final_g.reshape(1, D)).reshape(B, L, D)
```

```python
import functools
import math

import jax
import jax.numpy as jnp
import numpy as np
from jax import lax
from jax.experimental import pallas as pl
from jax.experimental.pallas import tpu as pltpu

F32 = jnp.float32
BF16 = jnp.bfloat16
HI = lax.Precision.HIGHEST
NORM_EPS = 1e-6
V7X_VMEM_LIMIT_BYTES = 56 * 1024 * 1024
N_MOD_ROWS = 16
NORM_ROWS = 256
SCAN_PAIR = 2

D_MODEL = 2048
HALF_W = D_MODEL // 2
GRID_W = 64
HY_W = HALF_W
HY_ORDER = 2
HY_BANDS = 16
HY_FAST_DECAY = 0.3
HY_SLOW_DECAY = 1.5
HY_DECAY_TARGET = 1e-2
GLA_HEADS = 4
GLA_DK = HALF_W // (2 * GLA_HEADS)
GLA_DV = HALF_W // GLA_HEADS
GLA_RANK = 16
GLA_TAU = 16.0
GLA_CHUNK = 64
ML_HEADS = 4
ML_DH = HALF_W // ML_HEADS
ML_CHUNK = 64
DA_HEADS = 8
DA_DH = HALF_W // (2 * DA_HEADS)
DA_DV = 2 * DA_DH
Q_BLOCK = 128
ROPE_BASE = 10000.0
N_EXPERTS = 16
EC_CAPACITY = 2
ML_IN = 4 * HALF_W + 4 * ML_HEADS


def _cparams(*sem, vmem_mib=None):
    limit = V7X_VMEM_LIMIT_BYTES if vmem_mib is None else vmem_mib * 1024 * 1024
    return pltpu.CompilerParams(dimension_semantics=sem, vmem_limit_bytes=limit)


def _dot(a, b):
    return jnp.dot(a, b, preferred_element_type=F32)


def _ada_kernel(c_ref, w_ref, b_ref, o_ref):
    c = c_ref[...]
    a = c * jax.nn.sigmoid(c)
    a_hi = a.astype(BF16)
    a_lo = (a - a_hi.astype(F32)).astype(BF16)
    w = w_ref[...]
    w_hi = w.astype(BF16)
    w_lo = (w - w_hi.astype(F32)).astype(BF16)
    n = a.shape[0]
    r1 = _dot(jnp.concatenate([a_hi, a_lo], axis=0), w_hi)
    r2 = _dot(a_hi, w_lo)
    o_ref[...] = r1[:n] + r1[n:] + r2 + b_ref[...]


def ada_mod(cvec, ada_w, ada_b, tn=512):
    n_lyr, d, n = ada_w.shape
    r = cvec.shape[0]
    return pl.pallas_call(
        _ada_kernel,
        grid=(n_lyr, n // tn),
        in_specs=[pl.BlockSpec((r, d), lambda l, j: (0, 0)),
                  pl.BlockSpec((None, d, tn), lambda l, j: (l, 0, j)),
                  pl.BlockSpec((None, 1, tn), lambda l, j: (l, 0, j))],
        out_specs=pl.BlockSpec((None, r, tn), lambda l, j: (l, 0, j)),
        out_shape=jax.ShapeDtypeStruct((n_lyr, r, n), F32),
        compiler_params=_cparams("parallel", "parallel"),
        name="ada_mod",
    )(cvec, ada_w, ada_b.reshape(n_lyr, 1, n))


def _mod_spec(d, chunk, row_fn, ngrid):
    if ngrid == 2:
        return pl.BlockSpec((None, 1, d), lambda i, j: (row_fn(i), 0, chunk))
    return pl.BlockSpec((None, 1, d), lambda i: (row_fn(i), 0, chunk))


def _norm_mod(x, g, shift, scale):
    y = x * lax.rsqrt(jnp.mean(x * x, axis=-1, keepdims=True) + NORM_EPS) * g
    return y * (1.0 + scale) + shift


def _inproj_kernel(h_ref, g_ref, sh_ref, sc_ref, w_ref, o_ref, a_scr):
    @pl.when(pl.program_id(1) == 0)
    def _():
        rows = min(NORM_ROWS, h_ref.shape[0])

        def body(r, carry):
            sl = pl.ds(pl.multiple_of(r * rows, rows), rows)
            a_scr[sl, :] = _norm_mod(h_ref[sl, :], g_ref[...], sh_ref[...], sc_ref[...]).astype(BF16)
            return carry

        lax.fori_loop(0, h_ref.shape[0] // rows, body, 0)

    o_ref[...] = _dot(a_scr[...], w_ref[...])


def in_proj(h, g, mod, row_fn, w, tm=2048, tn=512):
    t, d = h.shape
    n = w.shape[1]
    tm = min(tm, t)
    if n % tn:
        tn = 256 if n % 256 == 0 else 128
    return pl.pallas_call(
        _inproj_kernel,
        grid=(t // tm, n // tn),
        in_specs=[pl.BlockSpec((tm, d), lambda i, j: (i, 0), pipeline_mode=pl.Buffered(1)),
                  pl.BlockSpec((1, d), lambda i, j: (0, 0)),
                  _mod_spec(d, 0, row_fn, 2), _mod_spec(d, 1, row_fn, 2),
                  pl.BlockSpec((d, tn), lambda i, j: (0, j))],
        out_specs=pl.BlockSpec((tm, tn), lambda i, j: (i, j)),
        out_shape=jax.ShapeDtypeStruct((t, n), F32),
        scratch_shapes=[pltpu.VMEM((tm, d), BF16)],
        compiler_params=_cparams("parallel", "arbitrary", vmem_mib=48),
        name="in_proj",
    )(h, g, mod, mod, w)


def _outproj_kernel(ya_ref, yb_ref, w_ref, h_ref, gate_ref, o_ref):
    ka = ya_ref.shape[1]
    acc = _dot(ya_ref[...], w_ref[:ka, :]) + _dot(yb_ref[...], w_ref[ka:, :])
    o_ref[...] = h_ref[...] + gate_ref[...] * acc


def out_proj(ya, yb, w, h, mod, row_fn, tm=1024, tn=512):
    t, d = h.shape
    ka, kb = ya.shape[1], yb.shape[1]
    tm, tn = min(tm, t), min(tn, d)
    return pl.pallas_call(
        _outproj_kernel,
        grid=(t // tm, d // tn),
        in_specs=[pl.BlockSpec((tm, ka), lambda i, j: (i, 0)),
                  pl.BlockSpec((tm, kb), lambda i, j: (i, 0)),
                  pl.BlockSpec((ka + kb, tn), lambda i, j: (0, j)),
                  pl.BlockSpec((tm, tn), lambda i, j: (i, j)),
                  pl.BlockSpec((None, 1, tn), lambda i, j: (row_fn(i), 0, 2 * (d // tn) + j))],
        out_specs=pl.BlockSpec((tm, tn), lambda i, j: (i, j)),
        out_shape=jax.ShapeDtypeStruct((t, d), F32),
        compiler_params=_cparams("parallel", "parallel", vmem_mib=32),
        name="out_proj",
    )(ya, yb, w, h, mod)


def _moeprep_kernel(h_ref, g_ref, sh_ref, sc_ref, rw_ref, a_ref, lg_ref):
    a = _norm_mod(h_ref[...], g_ref[...], sh_ref[...], sc_ref[...])
    a_ref[...] = a.astype(BF16)
    lg_ref[...] = jnp.dot(a, rw_ref[...], precision=HI, preferred_element_type=F32)


def moe_prep(h, g, mod, row_fn, router_w, tm=512):
    t, d = h.shape
    e = router_w.shape[1]
    tm = min(tm, t)
    return pl.pallas_call(
        _moeprep_kernel,
        grid=(t // tm,),
        in_specs=[pl.BlockSpec((tm, d), lambda i: (i, 0)),
                  pl.BlockSpec((1, d), lambda i: (0, 0)),
                  _mod_spec(d, 3, row_fn, 1), _mod_spec(d, 4, row_fn, 1),
                  pl.BlockSpec((d, e), lambda i: (0, 0))],
        out_specs=[pl.BlockSpec((tm, d), lambda i: (i, 0)), pl.BlockSpec((tm, e), lambda i: (i, 0))],
        out_shape=[jax.ShapeDtypeStruct((t, d), BF16), jax.ShapeDtypeStruct((t, e), F32)],
        compiler_params=_cparams("parallel"),
        name="moe_prep",
    )(h, g, mod, mod, router_w)


def _route_kernel(lg_ref, pos_ref, gate_ref, *, cap):
    lg = lg_ref[...]
    ts, ne = lg.shape
    ex = jnp.exp(lg - jnp.max(lg, axis=-1, keepdims=True))
    aff = ex / jnp.sum(ex, axis=-1, keepdims=True)
    bits = pltpu.bitcast(aff, jnp.int32)

    def bisect(i, thr):
        cand = thr | jnp.left_shift(jnp.int32(1), 30 - i)
        cnt = jnp.sum((bits >= cand).astype(F32), axis=0, keepdims=True)
        return jnp.where(cnt >= cap, cand, thr)

    thr = lax.fori_loop(0, 31, bisect, jnp.zeros((1, ne), jnp.int32))
    gt = bits > thr
    eq = bits == thr
    n_gt = jnp.sum(gt.astype(F32), axis=0, keepdims=True)
    r = lax.broadcasted_iota(jnp.int32, (ts, ts), 0)
    c = lax.broadcasted_iota(jnp.int32, (ts, ts), 1)
    tri = (c < r).astype(BF16)
    eq_rank = _dot(tri, eq.astype(BF16))
    sel = gt | (eq & (eq_rank < cap - n_gt))
    pos = _dot(tri, sel.astype(BF16))
    pos_ref[...] = jnp.where(sel, pos, -1.0)
    gate_ref[...] = jnp.where(sel, aff, 0.0)


def moe_route(logits, n_samples, cap):
    t, e = logits.shape
    ts = t // n_samples
    spec = pl.BlockSpec((ts, e), lambda s: (s, 0))
    return pl.pallas_call(
        functools.partial(_route_kernel, cap=cap),
        grid=(n_samples,),
        in_specs=[spec],
        out_specs=[spec, spec],
        out_shape=[jax.ShapeDtypeStruct((t, e), F32)] * 2,
        compiler_params=_cparams("parallel"),
        name="moe_route",
    )(logits)


def _slot_onehot(pos_col, cap):
    slots = lax.broadcasted_iota(jnp.int32, (1, cap), 1).astype(F32)
    return (pos_col == slots).astype(BF16)


def _gather_kernel(pos_ref, a_ref, x_ref, *, cap):
    pos = pos_ref[...]
    a = a_ref[...]
    for e in range(pos.shape[1]):
        pt = _slot_onehot(pos[:, e:e + 1], cap)
        x = lax.dot_general(pt, a, (((0,), (0,)), ((), ())), preferred_element_type=F32)
        x_ref[e] = x.astype(BF16)


def moe_gather(pos, a, n_samples, cap, tn=1024):
    t, d = a.shape
    e = pos.shape[1]
    ts = t // n_samples
    tn = min(tn, d)
    return pl.pallas_call(
        functools.partial(_gather_kernel, cap=cap),
        grid=(n_samples, d // tn),
        in_specs=[pl.BlockSpec((ts, e), lambda s, j: (s, 0)), pl.BlockSpec((ts, tn), lambda s, j: (s, j))],
        out_specs=pl.BlockSpec((e, cap, tn), lambda s, j: (0, s, j)),
        out_shape=jax.ShapeDtypeStruct((e, n_samples * cap, d), BF16),
        compiler_params=_cparams("parallel", "parallel"),
        name="moe_gather",
    )(pos, a)


def _ffn_kernel(*refs, nx, nff):
    x_refs = refs[:nx]
    w1_ref, w3_ref, w2_ref = refs[nx:nx + 3]
    y_refs = refs[nx + 3:2 * nx + 3]
    hid_refs = refs[2 * nx + 3:]
    j = pl.program_id(1)

    @pl.when(j < nff)
    def _():
        w1 = w1_ref[...].astype(BF16)
        w3 = w3_ref[...].astype(BF16)
        for x_ref, hid_ref in zip(x_refs, hid_refs):
            x = x_ref[...]
            h1 = _dot(x, w1)
            h3 = _dot(x, w3)
            hid_ref[j] = (h1 * jax.nn.sigmoid(h1) * h3).astype(BF16)

    @pl.when(j >= nff)
    def _():
        w2 = w2_ref[...].astype(BF16)
        tf = w2.shape[0] // nff
        for y_ref, hid_ref in zip(y_refs, hid_refs):
            acc = _dot(hid_ref[0], w2[:tf])
            for jj in range(1, nff):
                acc += _dot(hid_ref[jj], w2[jj * tf:(jj + 1) * tf])
            y_ref[...] = acc.astype(y_ref.dtype)


def moe_ffn(xs, w1, w3, w2, layer, tf=256, tn=256):
    _, ne, d, ff = w1.shape
    tf, tn = min(tf, ff), min(tn, d)
    nff, nd = ff // tf, d // tn
    nx = len(xs)
    x_specs = [pl.BlockSpec((None, x.shape[1], d), lambda e, j: (e, 0, 0)) for x in xs]
    up_spec = pl.BlockSpec((None, None, d, tf), lambda e, j: (layer, e, 0, jnp.minimum(j, nff - 1)))
    down_spec = pl.BlockSpec((None, None, ff, tn), lambda e, j: (layer, e, 0, jnp.maximum(j - nff, 0)))
    y_specs = [pl.BlockSpec((None, x.shape[1], tn), lambda e, j: (e, 0, jnp.maximum(j - nff, 0))) for x in xs]
    return pl.pallas_call(
        functools.partial(_ffn_kernel, nx=nx, nff=nff),
        grid=(ne, nff + nd),
        in_specs=x_specs + [up_spec, up_spec, down_spec],
        out_specs=y_specs,
        out_shape=[jax.ShapeDtypeStruct(x.shape, BF16) for x in xs],
        scratch_shapes=[pltpu.VMEM((nff, x.shape[1], tf), BF16) for x in xs],
        compiler_params=_cparams("parallel", "arbitrary"),
        name="moe_ffn",
    )(*xs, w1, w3, w2)


def _combine_kernel(pos_ref, gate_ref, y_ref, h_ref, m_ref, o_ref, pt_scr, *, cap):
    ne = pos_ref.shape[1]

    @pl.when(pl.program_id(1) == 0)
    def _():
        pos = pos_ref[...]
        for e in range(ne):
            pt_scr[e] = _slot_onehot(pos[:, e:e + 1], cap)

    gate = gate_ref[...]
    acc = gate[:, 0:1] * _dot(pt_scr[0], y_ref[0])
    for e in range(1, ne):
        acc += gate[:, e:e + 1] * _dot(pt_scr[e], y_ref[e])
    o_ref[...] = h_ref[...] + m_ref[...] * acc


def moe_combine(pos, gate, y, h, mod, row_fn, n_samples, cap, tn=512):
    t, d = h.shape
    e = pos.shape[1]
    ts = t // n_samples
    tn = min(tn, d)
    return pl.pallas_call(
        functools.partial(_combine_kernel, cap=cap),
        grid=(n_samples, d // tn),
        in_specs=[pl.BlockSpec((ts, e), lambda s, j: (s, 0)),
                  pl.BlockSpec((ts, e), lambda s, j: (s, 0)),
                  pl.BlockSpec((e, cap, tn), lambda s, j: (0, s, j)),
                  pl.BlockSpec((ts, tn), lambda s, j: (s, j)),
                  pl.BlockSpec((None, 1, tn), lambda s, j: (row_fn(s), 0, 5 * (d // tn) + j))],
        out_specs=pl.BlockSpec((ts, tn), lambda s, j: (s, j)),
        out_shape=jax.ShapeDtypeStruct((t, d), F32),
        scratch_shapes=[pltpu.VMEM((e, ts, cap), BF16)],
        compiler_params=_cparams("parallel", "arbitrary"),
        name="moe_combine",
    )(pos, gate, y, h, mod)


def _rms_kernel(x_ref, g_ref, o_ref):
    x = x_ref[...]
    o_ref[...] = x * lax.rsqrt(jnp.mean(x * x, axis=-1, keepdims=True) + NORM_EPS) * g_ref[...]


def final_norm(h, g, tm=512):
    t, d = h.shape
    return pl.pallas_call(
        _rms_kernel,
        grid=(t // tm,),
        in_specs=[pl.BlockSpec((tm, d), lambda i: (i, 0)), pl.BlockSpec((1, d), lambda i: (0, 0))],
        out_specs=pl.BlockSpec((tm, d), lambda i: (i, 0)),
        out_shape=jax.ShapeDtypeStruct((t, d), F32),
        compiler_params=_cparams("parallel"),
        name="final_norm",
    )(h, g)


def tile_rows(row_of_sample, ts, tm):
    per = ts // tm
    return lambda i: row_of_sample(i // per)


def moe_layer(streams, g2, mod, router_w, w1, w3, w2, layer):
    routed = []
    for h, row_of_sample, n_samples in streams:
        ts = h.shape[0] // n_samples
        cap = EC_CAPACITY * ts // N_EXPERTS
        tm = min(512, ts)
        a, logits = moe_prep(h, g2, mod, tile_rows(row_of_sample, ts, tm), router_w, tm=tm)
        pos, gate = moe_route(logits, n_samples, cap)
        routed.append((pos, gate, cap, moe_gather(pos, a, n_samples, cap)))
    ys = moe_ffn([r[3] for r in routed], w1, w3, w2, layer)
    return [moe_combine(pos, gate, y, h, mod, row_of_sample, n_samples, cap)
            for (h, row_of_sample, n_samples), (pos, gate, cap, _), y in zip(streams, routed, ys)]


def _dir_tri(n, d):
    i = lax.broadcasted_iota(jnp.int32, (n, n), 0)
    j = lax.broadcasted_iota(jnp.int32, (n, n), 1)
    return (j - i) * (1 - 2 * d) <= 0


def _gla_chunk(q_ref, k_ref, v_ref, c_ref, w2_ref, gb_ref, o_ref, st_ref, d):
    cs = q_ref.shape[0]
    causal = _dir_tri(cs, d)
    logits = jnp.dot(c_ref[...], w2_ref[...], precision=HI, preferred_element_type=F32) + gb_ref[...]
    logg = jax.nn.log_sigmoid(logits) * (1.0 / GLA_TAU)
    b = jnp.dot(causal.astype(F32), logg, precision=HI, preferred_element_type=F32)
    b_last = jnp.sum(logg, axis=0, keepdims=True)
    q = q_ref[...] * (GLA_DK ** -0.5)
    k = k_ref[...]
    q_dec = (q * jnp.exp(b)).astype(BF16)
    k_dec = (k * jnp.exp(-b)).astype(BF16)
    k_end = (k * jnp.exp(b_last - b)).astype(BF16)
    v = v_ref[...].astype(BF16)
    for h in range(GLA_HEADS):
        ks = slice(h * GLA_DK, (h + 1) * GLA_DK)
        vs = slice(h * GLA_DV, (h + 1) * GLA_DV)
        st = st_ref[h]
        att = lax.dot_general(q_dec[:, ks], k_dec[:, ks], (((1,), (1,)), ((), ())), preferred_element_type=F32)
        att = jnp.where(causal, att, 0.0).astype(BF16)
        o = _dot(att, v[:, vs]) + lax.dot_general(q_dec[:, ks], st.astype(BF16), (((1,), (1,)), ((), ())),
                                                  preferred_element_type=F32)
        o_ref[:, vs] = o
        upd = lax.dot_general(v[:, vs], k_end[:, ks], (((0,), (0,)), ((), ())), preferred_element_type=F32)
        st_ref[h] = st * jnp.exp(b_last[:, ks]) + upd


def _gla_kernel(qc_ref, kc_ref, vc_ref, cc_ref, ql_ref, kl_ref, vl_ref, cl_ref, w2_ref, gb_ref,
                oc_ref, ol_ref, st_ref, *, ncc):
    d = pl.program_id(1)
    c = pl.program_id(2)

    @pl.when(c == 0)
    def _():
        st_ref[...] = jnp.zeros_like(st_ref)

    @pl.when(c < ncc)
    def _():
        for s in range(qc_ref.shape[0]):
            _gla_chunk(qc_ref.at[s], kc_ref.at[s], vc_ref.at[s], cc_ref.at[s], w2_ref, gb_ref, oc_ref.at[s],
                       st_ref.at[s], d)

    @pl.when(c >= ncc)
    def _():
        for s in range(ql_ref.shape[0]):
            _gla_chunk(ql_ref.at[s], kl_ref.at[s], vl_ref.at[s], cl_ref.at[s], w2_ref, gb_ref, ol_ref.at[s],
                       st_ref.at[s], d)


def _scan_chunk_maps(n_samples_chunks_ctx, n_samples_chunks_lat):
    ncc, nlc = n_samples_chunks_ctx, n_samples_chunks_lat

    def ctx_blk(b, d, c):
        i = jnp.minimum(c, ncc - 1)
        return b * ncc + jnp.where(d == 0, i, ncc - 1 - i)

    def lat_blk(b, d, c):
        i = jnp.maximum(c - ncc, 0)
        return b * nlc + jnp.where(d == 0, i, nlc - 1 - i)

    return ctx_blk, lat_blk


def gla_scan_call(p_ctx, p_lat, w2_full, gate_b, n_samples, col0, col_codes):
    cs = GLA_CHUNK
    hk, hv = GLA_HEADS * GLA_DK, GLA_HEADS * GLA_DV
    ncc, nlc = p_ctx.shape[0] // n_samples // cs, p_lat.shape[0] // n_samples // cs
    ctx_blk, lat_blk = _scan_chunk_maps(ncc, nlc)
    qb, kb, vb = col0 // hk, col0 // hk + 1, (col0 + 2 * hk) // hv
    cb = col_codes // LANES
    pair = SCAN_PAIR if n_samples % SCAN_PAIR == 0 else 1
    p_ctx = p_ctx.reshape(pair, p_ctx.shape[0] // pair, p_ctx.shape[1])
    p_lat = p_lat.reshape(pair, p_lat.shape[0] // pair, p_lat.shape[1])

    def specs(blk):
        return [pl.BlockSpec((pair, cs, hk), lambda b, d, c: (0, blk(b, d, c), qb)),
                pl.BlockSpec((pair, cs, hk), lambda b, d, c: (0, blk(b, d, c), kb)),
                pl.BlockSpec((pair, cs, hv), lambda b, d, c: (0, blk(b, d, c), vb)),
                pl.BlockSpec((pair, cs, 128), lambda b, d, c: (0, blk(b, d, c), cb))]

    o_ctx, o_lat = pl.pallas_call(
        functools.partial(_gla_kernel, ncc=ncc),
        grid=(n_samples // pair, 2, ncc + nlc),
        in_specs=specs(ctx_blk) + specs(lat_blk) + [
            pl.BlockSpec((None, 128, hk), lambda b, d, c: (d, 0, 0)),
            pl.BlockSpec((None, 1, hk), lambda b, d, c: (d, 0, 0))],
        out_specs=[pl.BlockSpec((None, pair, cs, hv), lambda b, d, c: (d, 0, ctx_blk(b, d, c), 0)),
                   pl.BlockSpec((None, pair, cs, hv), lambda b, d, c: (d, 0, lat_blk(b, d, c), 0))],
        out_shape=[jax.ShapeDtypeStruct((2,) + p_ctx.shape[:2] + (hv,), F32),
                   jax.ShapeDtypeStruct((2,) + p_lat.shape[:2] + (hv,), F32)],
        scratch_shapes=[pltpu.VMEM((pair, GLA_HEADS, GLA_DV, GLA_DK), F32)],
        compiler_params=_cparams("parallel", "arbitrary", "arbitrary", vmem_mib=32),
        name="gla_scan",
    )(p_ctx, p_ctx, p_ctx, p_ctx, p_lat, p_lat, p_lat, p_lat, w2_full, gate_b)
    return o_ctx.reshape(2, -1, hv), o_lat.reshape(2, -1, hv)


def _headnorm_gate_kernel(o_ref, r_ref, g_ref, y_ref, *, n_heads, act):
    o = o_ref[0] + o_ref[1]
    r = r_ref[...]
    gate = r * jax.nn.sigmoid(r) if act == "silu" else jax.nn.sigmoid(r)
    dh = o.shape[1] // n_heads
    for h in range(n_heads):
        s = slice(h * dh, (h + 1) * dh)
        oh = o[:, s]
        yh = oh * lax.rsqrt(jnp.mean(oh * oh, axis=-1, keepdims=True) + NORM_EPS) * g_ref[...]
        y_ref[:, s] = (yh * gate[:, s]).astype(y_ref.dtype)


def headnorm_gate(o2, p, r_col, norm_g, n_heads, act, tm=512):
    _, t, w = o2.shape
    tm = min(tm, t)
    return pl.pallas_call(
        functools.partial(_headnorm_gate_kernel, n_heads=n_heads, act=act),
        grid=(t // tm,),
        in_specs=[pl.BlockSpec((2, tm, w), lambda i: (0, i, 0)),
                  pl.BlockSpec((tm, w), lambda i: (i, r_col // w)),
                  pl.BlockSpec((1, w // n_heads), lambda i: (0, 0))],
        out_specs=pl.BlockSpec((tm, w), lambda i: (i, 0)),
        out_shape=jax.ShapeDtypeStruct((t, w), BF16),
        compiler_params=_cparams("parallel", vmem_mib=32),
        name="headnorm_gate",
    )(o2, p, norm_g.reshape(1, -1))


def gla_mixer(p_ctx, p_lat, gate_w2, gate_b, norm_g, n_samples, col0):
    hk, hv = GLA_HEADS * GLA_DK, GLA_HEADS * GLA_DV
    w2_full = jnp.zeros((2, 128, hk), F32)
    for d in range(2):
        w2_full = w2_full.at[d, d * GLA_RANK:(d + 1) * GLA_RANK].set(gate_w2[d])
    r_col = col0 + 2 * hk + hv
    o_ctx, o_lat = gla_scan_call(p_ctx, p_lat, w2_full, gate_b.reshape(2, 1, hk), n_samples, col0, r_col + hv)
    return (headnorm_gate(o_ctx, p_ctx, r_col, norm_g, GLA_HEADS, "silu"),
            headnorm_gate(o_lat, p_lat, r_col, norm_g, GLA_HEADS, "silu"))


ROPE_NF = DA_DH // 4
LANES = 128


def _rope_table_kernel(cos_ref, sin_ref):
    n = cos_ref.shape[0]
    t = lax.broadcasted_iota(jnp.int32, (n, LANES), 0)
    j = lax.broadcasted_iota(jnp.int32, (n, LANES), 1)
    dd = j % DA_DH
    is_col = (dd // (2 * ROPE_NF)) == 1
    is_xb = ((dd // ROPE_NF) % 2) == 1
    f = (dd % ROPE_NF).astype(F32)
    inv = jnp.exp(f * (-math.log(ROPE_BASE) / ROPE_NF))
    pos = jnp.where(is_col, t % GRID_W, t // GRID_W).astype(F32)
    ang = pos * inv
    cos_ref[...] = jnp.cos(ang)
    sin_ref[...] = jnp.where(is_xb, 1.0, -1.0) * jnp.sin(ang)


def rope_tables(n):
    out = jax.ShapeDtypeStruct((n, LANES), F32)
    return pl.pallas_call(_rope_table_kernel, out_shape=[out, out], name="rope_tables")()


def _rope(x, cos, sin_signed):
    lane = lax.broadcasted_iota(jnp.int32, x.shape, 1)
    is_xb = ((lane // ROPE_NF) % 2) == 1
    partner = jnp.where(is_xb, pltpu.roll(x, ROPE_NF, 1), pltpu.roll(x, LANES - ROPE_NF, 1))
    return x * cos + partner * sin_signed


def _nt(a, b):
    return lax.dot_general(a, b, (((1,), (1,)), ((), ())), preferred_element_type=F32)


def _diffattn_kernel(q_ref, kl_ref, vl_ref, kc_ref, vc_ref, cq_ref, sq_ref, ck_ref, sk_ref, lam_ref, g_ref, o_ref,
                     kl_scr, vl_scr, kc_scr, vc_scr, *, lam_init):
    @pl.when(pl.program_id(2) == 0)
    def _():
        kl_scr[...] = _rope(kl_ref[...], ck_ref[...], sk_ref[...]).astype(BF16)
        kc_scr[...] = kc_ref[...].astype(BF16)
        vl_scr[:, :LANES] = vl_ref[...].astype(BF16)
        vl_scr[:, LANES:] = jnp.ones(vl_ref.shape, BF16)
        vc_scr[:, :LANES] = vc_ref[...].astype(BF16)
        vc_scr[:, LANES:] = jnp.ones(vc_ref.shape, BF16)

    lv = lam_ref[...]
    lam = (jnp.exp(jnp.sum(lv[0:1] * lv[1:2], keepdims=True)) - jnp.exp(jnp.sum(lv[2:3] * lv[3:4], keepdims=True))
           + lam_init)
    q = _rope(q_ref[...], cq_ref[...], sq_ref[...]) * (DA_DH ** -0.5)
    first = lax.broadcasted_iota(jnp.int32, q.shape, 1) < DA_DH
    attn = []
    for comp in range(2):
        qc = jnp.where(first == (comp == 0), q, 0.0).astype(BF16)
        s_l = _nt(qc, kl_scr[...])
        s_c = _nt(qc, kc_scr[...])
        m = jnp.maximum(jnp.max(s_l, axis=-1, keepdims=True), jnp.max(s_c, axis=-1, keepdims=True))
        acc = _dot(jnp.exp(s_l - m).astype(BF16), vl_scr[...]) + _dot(jnp.exp(s_c - m).astype(BF16), vc_scr[...])
        attn.append(acc[:, :LANES] / acc[:, LANES:])
    o = attn[0] - lam * attn[1]
    o = o * lax.rsqrt(jnp.mean(o * o, axis=-1, keepdims=True) + NORM_EPS) * g_ref[...]
    o_ref[...] = (o * (1.0 - lam_init)).astype(o_ref.dtype)


def diff_attention_call(p_ctx, p_lat, n_samples, col_q, lam_vecs, norm_g, lam_init, tq=512):
    lt, lc = p_lat.shape[0] // n_samples, p_ctx.shape[0] // n_samples
    tq = min(tq, lt)
    nq = lt // tq
    w = DA_HEADS * DA_DV
    qb, kb, vb = col_q // LANES, (col_q + w) // LANES, (col_q + 2 * w) // LANES
    cos, sin = rope_tables(lt)
    return pl.pallas_call(
        functools.partial(_diffattn_kernel, lam_init=lam_init),
        grid=(n_samples, DA_HEADS, nq),
        in_specs=[pl.BlockSpec((tq, LANES), lambda b, h, i: (b * nq + i, qb + h)),
                  pl.BlockSpec((lt, LANES), lambda b, h, i: (b, kb + h)),
                  pl.BlockSpec((lt, LANES), lambda b, h, i: (b, vb + h)),
                  pl.BlockSpec((lc, LANES), lambda b, h, i: (b, kb + h)),
                  pl.BlockSpec((lc, LANES), lambda b, h, i: (b, vb + h)),
                  pl.BlockSpec((tq, LANES), lambda b, h, i: (i, 0)),
                  pl.BlockSpec((tq, LANES), lambda b, h, i: (i, 0)),
                  pl.BlockSpec((lt, LANES), lambda b, h, i: (0, 0)),
                  pl.BlockSpec((lt, LANES), lambda b, h, i: (0, 0)),
                  pl.BlockSpec(lam_vecs.shape, lambda b, h, i: (0, 0)),
                  pl.BlockSpec((1, DA_DV), lambda b, h, i: (0, 0))],
        out_specs=pl.BlockSpec((tq, LANES), lambda b, h, i: (b * nq + i, h)),
        out_shape=jax.ShapeDtypeStruct((p_lat.shape[0], w), BF16),
        scratch_shapes=[pltpu.VMEM((lt, LANES), BF16), pltpu.VMEM((lt, 2 * LANES), BF16),
                        pltpu.VMEM((lc, LANES), BF16), pltpu.VMEM((lc, 2 * LANES), BF16)],
        compiler_params=_cparams("parallel", "parallel", "arbitrary", vmem_mib=48),
        name="diff_attention",
    )(p_lat, p_lat, p_lat, p_ctx, p_ctx, cos, sin, cos, sin, lam_vecs, norm_g.reshape(1, DA_DV))


def _convsilu_kernel(u_ref, w_ref, b_ref, s_ref, o_ref, *, silu):
    u = u_ref[...]
    n = u.shape[0]
    row = lax.broadcasted_iota(jnp.int32, u.shape, 0)
    prev = jnp.where(row == 0, 0.0, pltpu.roll(u, 1, 0))
    nxt = jnp.where(row == n - 1, 0.0, pltpu.roll(u, n - 1, 0))
    w = w_ref[...]
    y = w[0:1] * prev + w[1:2] * u + w[2:3] * nxt + b_ref[...]
    if silu:
        y = y * jax.nn.sigmoid(y)
    o_ref[...] = y * s_ref[...]


def conv_silu(p, n_samples, width, conv_w, conv_b, col_scale, tn=512, silu=True):
    ls = p.shape[0] // n_samples
    return pl.pallas_call(
        functools.partial(_convsilu_kernel, silu=silu),
        grid=(n_samples, width // tn),
        in_specs=[pl.BlockSpec((ls, tn), lambda s, j: (s, j)),
                  pl.BlockSpec((conv_w.shape[0], tn), lambda s, j: (0, j)),
                  pl.BlockSpec((1, tn), lambda s, j: (0, j)),
                  pl.BlockSpec((1, tn), lambda s, j: (0, j))],
        out_specs=pl.BlockSpec((ls, tn), lambda s, j: (s, j)),
        out_shape=jax.ShapeDtypeStruct((p.shape[0], width), F32),
        compiler_params=_cparams("parallel", "parallel", vmem_mib=32),
        name="conv_silu",
    )(p, conv_w, conv_b.reshape(1, -1), col_scale.reshape(1, -1))


def _mlstm_chunk(q_ref, k_ref, v_ref, g_ref, gb_ref, o_ref, c_ref, n_ref, m_ref, d, emit):
    cs = q_ref.shape[0]
    causal = _dir_tri(cs, d)
    g = g_ref[...] + gb_ref[...]
    lane = lax.broadcasted_iota(jnp.int32, g.shape, 1)
    gl = jnp.where((lane // ML_HEADS) % 2 == 1, jax.nn.log_sigmoid(g), g)
    bcum = jnp.dot(causal.astype(F32), gl, precision=HI, preferred_element_type=F32)
    gl_t = gl.T
    bcum_t = bcum.T
    q = q_ref[...].astype(BF16)
    k = k_ref[...]
    kb = k.astype(BF16)
    v = v_ref[...]

    def pick_col(a, idx):
        return jnp.where(d == 0, a[:, idx:idx + 1], a[:, 2 * ML_HEADS + idx:2 * ML_HEADS + idx + 1])

    def pick_row(a, idx):
        return jnp.where(d == 0, a[idx:idx + 1, :], a[2 * ML_HEADS + idx:2 * ML_HEADS + idx + 1, :])

    for h in range(ML_HEADS):
        hs = slice(h * ML_DH, (h + 1) * ML_DH)
        ic_c, ic_r = pick_col(gl, h), pick_row(gl_t, h)
        fc_c = pick_col(gl, ML_HEADS + h)
        b_c, b_r = pick_col(bcum, ML_HEADS + h), pick_row(bcum_t, ML_HEADS + h)
        b_last = jnp.sum(fc_c, axis=0, keepdims=True)
        m = m_ref[h][:, 0:1]
        cm = c_ref[h]
        nv = n_ref[h]
        w_end_c = b_last - b_c + ic_c
        w_end_r = b_last - b_r + ic_r
        m_new = jnp.maximum(b_last + m, jnp.max(w_end_r, axis=-1, keepdims=True))
        keep = jnp.exp(b_last + m - m_new)
        w_c = jnp.exp(w_end_c - m_new)
        vw = (v[:, hs] * w_c).astype(BF16)
        c_ref[h] = keep * cm + lax.dot_general(vw, kb[:, hs], (((0,), (0,)), ((), ())), preferred_element_type=F32)
        n_ref[h] = keep * nv + jnp.sum(w_c * k[:, hs], axis=0, keepdims=True)
        m_ref[h] = jnp.broadcast_to(m_new, m_ref.shape[1:])
        if emit:
            a_c = b_c + m
            dlog = jnp.where(causal, b_c - b_r + ic_r, -jnp.inf)
            m_t = jnp.maximum(a_c, jnp.max(dlog, axis=-1, keepdims=True))
            sc = _nt(q[:, hs], kb[:, hs]) * jnp.exp(dlog - m_t)
            aw = jnp.exp(a_c - m_t)
            num = _dot(sc.astype(BF16), v[:, hs].astype(BF16)) + aw * _nt(q[:, hs], cm.astype(BF16))
            qf = q_ref[:, hs]
            den = jnp.sum(sc, axis=-1, keepdims=True) + aw * jnp.sum(qf * nv, axis=-1, keepdims=True)
            o_ref[:, hs] = num / jnp.maximum(jnp.abs(den), jnp.exp(-m_t))


def _mlstm_kernel(qc_ref, kc_ref, vc_ref, gc_ref, ql_ref, kl_ref, vl_ref, gl_ref, gb_ref,
                  ol_ref, c_ref, n_ref, m_ref, *, ncc):
    d = pl.program_id(1)
    c = pl.program_id(2)

    @pl.when(c == 0)
    def _():
        c_ref[...] = jnp.zeros_like(c_ref)
        n_ref[...] = jnp.zeros_like(n_ref)
        m_ref[...] = jnp.zeros_like(m_ref)

    @pl.when(c < ncc)
    def _():
        for s in range(qc_ref.shape[0]):
            _mlstm_chunk(qc_ref.at[s], kc_ref.at[s], vc_ref.at[s], gc_ref.at[s], gb_ref, None,
                         c_ref.at[s], n_ref.at[s], m_ref.at[s], d, False)

    @pl.when(c >= ncc)
    def _():
        for s in range(ql_ref.shape[0]):
            _mlstm_chunk(ql_ref.at[s], kl_ref.at[s], vl_ref.at[s], gl_ref.at[s], gb_ref, ol_ref.at[s],
                         c_ref.at[s], n_ref.at[s], m_ref.at[s], d, True)


def mlstm_scan_call(qk_ctx, qk_lat, p_ctx, p_lat, gate_bias, n_samples, col_v, col_g):
    cs = ML_CHUNK
    w = ML_HEADS * ML_DH
    ncc, nlc = p_ctx.shape[0] // n_samples // cs, p_lat.shape[0] // n_samples // cs
    ctx_blk, lat_blk = _scan_chunk_maps(ncc, nlc)
    vb = col_v // w
    gcol = col_g // LANES
    pair = SCAN_PAIR if n_samples % SCAN_PAIR == 0 else 1
    halves = lambda a: a.reshape(pair, a.shape[0] // pair, a.shape[1])
    qk_ctx, qk_lat, p_ctx, p_lat = halves(qk_ctx), halves(qk_lat), halves(p_ctx), halves(p_lat)

    def specs(blk):
        return [pl.BlockSpec((pair, cs, w), lambda b, d, c: (0, blk(b, d, c), 0)),
                pl.BlockSpec((pair, cs, w), lambda b, d, c: (0, blk(b, d, c), 1)),
                pl.BlockSpec((pair, cs, w), lambda b, d, c: (0, blk(b, d, c), vb)),
                pl.BlockSpec((pair, cs, LANES), lambda b, d, c: (0, blk(b, d, c), gcol))]

    out = pl.pallas_call(
        functools.partial(_mlstm_kernel, ncc=ncc),
        grid=(n_samples // pair, 2, ncc + nlc),
        in_specs=specs(ctx_blk) + specs(lat_blk) + [pl.BlockSpec((1, LANES), lambda b, d, c: (0, 0))],
        out_specs=pl.BlockSpec((None, pair, cs, w), lambda b, d, c: (d, 0, lat_blk(b, d, c), 0)),
        out_shape=jax.ShapeDtypeStruct((2,) + p_lat.shape[:2] + (w,), F32),
        scratch_shapes=[pltpu.VMEM((pair, ML_HEADS, ML_DH, ML_DH), F32), pltpu.VMEM((pair, ML_HEADS, 1, ML_DH), F32),
                        pltpu.VMEM((pair, ML_HEADS, 1, LANES), F32)],
        compiler_params=_cparams("parallel", "arbitrary", "arbitrary", vmem_mib=32),
        name="mlstm_scan",
    )(qk_ctx, qk_ctx, p_ctx, p_ctx, qk_lat, qk_lat, p_lat, p_lat, gate_bias)
    return out.reshape(2, -1, w)


def mlstm_mixer(p_ctx, p_lat, conv_w, conv_b, igate_b, fgate_b, norm_g, n_samples, col_g):
    w = ML_HEADS * ML_DH
    scale = jnp.concatenate([jnp.ones((w,), F32), jnp.full((w,), ML_DH ** -0.5, F32)])
    qk_ctx = conv_silu(p_ctx, n_samples, 2 * w, conv_w, conv_b, scale)
    qk_lat = conv_silu(p_lat, n_samples, 2 * w, conv_w, conv_b, scale)
    bias = jnp.concatenate([jnp.stack([igate_b[d], fgate_b[d]]).reshape(-1) for d in range(2)])
    gate_bias = jnp.zeros((1, LANES), F32).at[0, :bias.shape[0]].set(bias)
    o2 = mlstm_scan_call(qk_ctx, qk_lat, p_ctx, p_lat, gate_bias, n_samples, 2 * w, col_g)
    return headnorm_gate(o2, p_lat, 3 * w, norm_g, ML_HEADS, "sigmoid")


HY_FEAT_PAD = 128
DFT_FREQ_PAD = 128


def _hyfilt_kernel(w1_ref, b1_ref, fr_ref, w2_ref, b2_ref, w3_ref, o_ref, *, seq):
    tn = o_ref.shape[1]
    pos = lax.broadcasted_iota(jnp.int32, (seq, HY_FEAT_PAD), 0).astype(F32)
    lane = lax.broadcasted_iota(jnp.int32, (seq, HY_FEAT_PAD), 1)
    t = pos / max(seq - 1, 1)
    band = 1e-4 + ((lane - 1) % HY_BANDS).astype(F32) * ((HY_BANDS - 1 - 1e-4) / (HY_BANDS - 1))
    ang = (2.0 * math.pi / seq) * pos * band
    feats = jnp.where(lane == 0, t, jnp.where(lane <= HY_BANDS, jnp.cos(ang),
                                              jnp.where(lane <= 2 * HY_BANDS, -jnp.sin(ang), 0.0)))
    fr = fr_ref[...]
    h = jnp.sin(fr[0:1] * (jnp.dot(feats, w1_ref[...], precision=HI, preferred_element_type=F32) + b1_ref[...]))
    h = jnp.sin(fr[1:2] * (jnp.dot(h, w2_ref[...], precision=HI, preferred_element_type=F32) + b2_ref[...]))
    h = jnp.dot(h, w3_ref[...], precision=HI, preferred_element_type=F32)
    col = pl.program_id(0) * tn + lax.broadcasted_iota(jnp.int32, (1, tn), 1)
    chan = col % HY_W
    lo, hi = math.log(HY_DECAY_TARGET) / HY_SLOW_DECAY, math.log(HY_DECAY_TARGET) / HY_FAST_DECAY
    delta = jnp.abs(lo + chan.astype(F32) * ((hi - lo) / (HY_W - 1)))
    filt = h * jnp.exp(-t[:, 0:1] * delta)
    is_bwd = (col // HY_W) % 2 == 1
    o_ref[...] = jnp.where(is_bwd & (pos[:, 0:1] == 0.0), 0.0, filt)


def hyena_filters_call(seq, w1, b1, freq, w2, b2, w3, tn=512):
    nf = w1.shape[1]
    n = w3.shape[1]
    w1p = jnp.zeros((HY_FEAT_PAD, nf), F32).at[:w1.shape[0]].set(w1)
    full = lambda shape: pl.BlockSpec(shape, lambda j: (0, 0))
    return pl.pallas_call(
        functools.partial(_hyfilt_kernel, seq=seq),
        grid=(n // tn,),
        in_specs=[full((HY_FEAT_PAD, nf)), full((1, nf)), full((2, nf)), full((nf, nf)), full((1, nf)),
                  pl.BlockSpec((nf, tn), lambda j: (0, j))],
        out_specs=pl.BlockSpec((seq, tn), lambda j: (0, j)),
        out_shape=jax.ShapeDtypeStruct((seq, n), F32),
        compiler_params=_cparams("parallel", vmem_mib=32),
        name="hyena_filters",
    )(w1p, b1.reshape(1, nf), freq, w2, b2.reshape(1, nf), w3)


def _dft_trig(f, s, n, minus_sin):
    k = (f * s + jnp.where(minus_sin, n // 4, 0)) % n
    return jnp.cos(k.astype(F32) * (2.0 * math.pi / n))


def _dft_fwd_kernel(o_ref, *, seq, nfreq):
    tr = o_ref.shape[0]
    r = pl.program_id(0) * tr + lax.broadcasted_iota(jnp.int32, (tr, seq), 0)
    s = lax.broadcasted_iota(jnp.int32, (tr, seq), 1)
    f = r % nfreq
    val = _dft_trig(f, s, 2 * seq, r >= nfreq)
    o_ref[...] = jnp.where(f <= seq, val, 0.0).astype(o_ref.dtype)


def _dft_inv_kernel(o_ref, *, seq, nfreq):
    tc = o_ref.shape[1]
    c = pl.program_id(0) * tc + lax.broadcasted_iota(jnp.int32, (seq, tc), 1)
    t = lax.broadcasted_iota(jnp.int32, (seq, tc), 0)
    f = c % nfreq
    wf = jnp.where((f == 0) | (f == seq), 1.0, jnp.where(f < seq, 2.0, 0.0)) * (0.5 / seq)
    o_ref[...] = (wf * _dft_trig(f, t, 2 * seq, c >= nfreq)).astype(o_ref.dtype)


def dft_tables(seq):
    nfreq = seq + DFT_FREQ_PAD
    fwd = pl.pallas_call(
        functools.partial(_dft_fwd_kernel, seq=seq, nfreq=nfreq), grid=(2 * nfreq // 128,),
        out_specs=pl.BlockSpec((128, seq), lambda i: (i, 0)),
        out_shape=jax.ShapeDtypeStruct((2 * nfreq, seq), BF16), compiler_params=_cparams("parallel", vmem_mib=32),
        name="dft_fwd_table")()
    inv = pl.pallas_call(
        functools.partial(_dft_inv_kernel, seq=seq, nfreq=nfreq), grid=(2 * nfreq // 128,),
        out_specs=pl.BlockSpec((seq, 128), lambda j: (0, j)),
        out_shape=jax.ShapeDtypeStruct((seq, 2 * nfreq), BF16), compiler_params=_cparams("parallel", vmem_mib=32),
        name="dft_inv_table")()
    return fwd, inv


def _dft_apply_kernel(a_ref, x_ref, o_ref, xb_scr):
    @pl.when(pl.program_id(2) == 0)
    def _():
        xb_scr[...] = x_ref[...].astype(BF16)

    o_ref[...] = _dot(a_ref[...], xb_scr[...])


def dft_apply(table, x, n_samples, col0, width, tn=512):
    m, ls = table.shape
    tn = min(tn, width)
    tm = m // 2
    return pl.pallas_call(
        _dft_apply_kernel,
        grid=(n_samples, width // tn, m // tm),
        in_specs=[pl.BlockSpec((tm, ls), lambda s, j, i: (i, 0)),
                  pl.BlockSpec((ls, tn), lambda s, j, i: (s, col0 // tn + j))],
        out_specs=pl.BlockSpec((None, tm, tn), lambda s, j, i: (s, i, j)),
        out_shape=jax.ShapeDtypeStruct((n_samples, m, width), F32),
        scratch_shapes=[pltpu.VMEM((ls, tn), BF16)],
        compiler_params=_cparams("parallel", "parallel", "arbitrary", vmem_mib=48),
        name="dft_apply",
    )(table, x)


def _conv_back_kernel(a_ref, z_ref, hf_ref, hb_ref, gate_ref, prev_ref, skip_ref, o_ref, y_scr):
    nfreq = z_ref.shape[0] // 2

    @pl.when(pl.program_id(2) == 0)
    def _():
        ure, uim = z_ref[:nfreq], z_ref[nfreq:]
        hre = hf_ref[0, :nfreq] + hb_ref[0, :nfreq]
        him = hf_ref[0, nfreq:] - hb_ref[0, nfreq:]
        y_scr[:nfreq] = (ure * hre - uim * him).astype(BF16)
        y_scr[nfreq:] = (ure * him + uim * hre).astype(BF16)

    y = _dot(a_ref[...], y_scr[...])
    prev = prev_ref[...]
    o_ref[...] = (gate_ref[...] * (y + skip_ref[...] * prev)).astype(o_ref.dtype)


def conv_back(inv_table, z, hspec, order, gate, gate_col, prev, prev_col, skip, out_dtype, tm=1024, tn=256):
    ns, m2, c = z.shape
    ls = inv_table.shape[0]
    tm = min(tm, ls)
    hcol = order * 2 * c // tn
    return pl.pallas_call(
        _conv_back_kernel,
        grid=(ns, c // tn, ls // tm),
        in_specs=[pl.BlockSpec((tm, m2), lambda s, j, i: (i, 0)),
                  pl.BlockSpec((None, m2, tn), lambda s, j, i: (s, 0, j)),
                  pl.BlockSpec((1, m2, tn), lambda s, j, i: (0, 0, hcol + j)),
                  pl.BlockSpec((1, m2, tn), lambda s, j, i: (0, 0, hcol + c // tn + j)),
                  pl.BlockSpec((tm, tn), lambda s, j, i: (s * (ls // tm) + i, gate_col // tn + j)),
                  pl.BlockSpec((tm, tn), lambda s, j, i: (s * (ls // tm) + i, prev_col // tn + j)),
                  pl.BlockSpec((1, tn), lambda s, j, i: (0, j))],
        out_specs=pl.BlockSpec((tm, tn), lambda s, j, i: (s * (ls // tm) + i, j)),
        out_shape=jax.ShapeDtypeStruct((ns * ls, c), out_dtype),
        scratch_shapes=[pltpu.VMEM((m2, tn), BF16)],
        compiler_params=_cparams("parallel", "parallel", "arbitrary"),
        name="conv_back",
    )(inv_table, z, hspec, hspec, gate, prev, skip)


def hyena_mixer(p, n_samples, conv_w, conv_b, filt_params, skip):
    ls = p.shape[0] // n_samples
    c = HY_W
    fwd, inv = dft_tables(ls)
    filt = hyena_filters_call(ls, *filt_params)
    hspec = dft_apply(fwd, filt, 1, 0, filt.shape[1])
    xc = conv_silu(p, n_samples, 3 * c, conv_w, conv_b, jnp.ones((3 * c,), F32), silu=False)
    v, v_col = xc, 2 * c
    for order in range(HY_ORDER):
        z = dft_apply(fwd, v, n_samples, v_col, c)
        last = order == HY_ORDER - 1
        v = conv_back(inv, z, hspec, order, xc, order * c, v, v_col, skip[order].reshape(1, c),
                      BF16 if last else F32)
        v_col = 0
    return v


def _pad_cols(w, mult=128):
    n = w.shape[-1]
    return jnp.pad(w, ((0, 0), (0, (-n) % mult)))


def kernel(x, c, ctx, c_ctx, ada_w, ada_b, norm1_g, norm2_g, out_w, router_w, moe_w1, moe_w3, moe_w2, final_g, ev_in_w, hy_conv_w, hy_conv_b, hy_pos_w1, hy_pos_b1, hy_sin_freq, hy_pos_w2, hy_pos_b2, hy_pos_w3, hy_bias, gla_gate_w2, gla_gate_b, gla_norm_g, od_in_w, ml_conv_w, ml_conv_b, ml_igate_b, ml_fgate_b, ml_norm_g, da_lambda, da_norm_g):
    B, L, D = x.shape
    Lc = ctx.shape[1]
    depth = ada_w.shape[0]
    cvec = jnp.zeros((N_MOD_ROWS, D), F32).at[:B].set(c).at[B].set(c_ctx)
    mods = ada_mod(cvec, ada_w, ada_b)
    lat_row = lambda s: s
    ctx_row = lambda s: s * 0 + B
    h_x = x.reshape(B * L, D)
    h_c = ctx.reshape(B * Lc, D)
    tm_lat = min(1024, L)
    for l in range(depth):
        last = l == depth - 1
        mod = mods[l].reshape(N_MOD_ROWS, 1, 6 * D)
        g1 = norm1_g[l].reshape(1, D)
        if l % 2 == 0:
            in_w = ev_in_w[l // 2]
        else:
            od = od_in_w[l // 2]
            ml_w = 4 * HALF_W
            in_w = jnp.concatenate([od[:, :ml_w], od[:, ML_IN:], od[:, ml_w:ML_IN]], axis=1)
        w_in = _pad_cols(in_w, 512).astype(BF16)
        w_out = out_w[l].astype(BF16)
        p_lat2 = in_proj(h_x, g1, mod, lat_row, w_in, tm=L)
        p_ctx2 = in_proj(h_c, g1, mod, ctx_row, w_in, tm=B * Lc)
        if l % 2 == 0:
            e = l // 2
            hy_n = 3 * HY_W
            filt_params = (hy_pos_w1[e], hy_pos_b1[e], hy_sin_freq[e], hy_pos_w2[e], hy_pos_b2[e], hy_pos_w3[e])
            ya_l = hyena_mixer(p_lat2, B, hy_conv_w[e], hy_conv_b[e], filt_params, hy_bias[e])
            ya_c = hyena_mixer(p_ctx2, B, hy_conv_w[e], hy_conv_b[e], filt_params, hy_bias[e])
            yb_c, yb_l = gla_mixer(p_ctx2, p_lat2, gla_gate_w2[e], gla_gate_b[e], gla_norm_g[e], B, hy_n)
        else:
            o = l // 2
            lam_init = 0.8 - 0.6 * math.exp(-0.3 * l)
            ya_l = mlstm_mixer(p_ctx2, p_lat2, ml_conv_w[o], ml_conv_b[o], ml_igate_b[o], ml_fgate_b[o], ml_norm_g[o],
                               B, ml_w + 3 * HALF_W)
            yb_l = diff_attention_call(p_ctx2, p_lat2, B, ml_w, da_lambda[o], da_norm_g[o], lam_init)
        h_x = out_proj(ya_l, yb_l, w_out, h_x, mod, tile_rows(lat_row, L, tm_lat), tm=tm_lat)
        streams = [(h_x, lat_row, B)]
        if not last:
            h_c = out_proj(ya_c, yb_c, w_out, h_c, mod, tile_rows(ctx_row, Lc, Lc), tm=Lc)
            streams.append((h_c, ctx_row, B))
        new = moe_layer(streams, norm2_g[l].reshape(1, D), mod, router_w[l], moe_w1, moe_w3, moe_w2, l)
        h_x = new[0]
        if not last:
            h_c = new[1]
    return final_norm(h_x, final_g.reshape(1, D)).reshape(B, L, D)
```

```python
import functools
import math

import jax
import jax.numpy as jnp
import numpy as np
from jax import lax
from jax.experimental import pallas as pl
from jax.experimental.pallas import tpu as pltpu

F32 = jnp.float32
BF16 = jnp.bfloat16
HI = lax.Precision.HIGHEST
NORM_EPS = 1e-6
V7X_VMEM_LIMIT_BYTES = 56 * 1024 * 1024
N_MOD_ROWS = 16
NORM_ROWS = 256
SCAN_PAIR = 2

D_MODEL = 2048
HALF_W = D_MODEL // 2
GRID_W = 64
HY_W = HALF_W
HY_ORDER = 2
HY_BANDS = 16
HY_FAST_DECAY = 0.3
HY_SLOW_DECAY = 1.5
HY_DECAY_TARGET = 1e-2
GLA_HEADS = 4
GLA_DK = HALF_W // (2 * GLA_HEADS)
GLA_DV = HALF_W // GLA_HEADS
GLA_RANK = 16
GLA_TAU = 16.0
GLA_CHUNK = 64
ML_HEADS = 4
ML_DH = HALF_W // ML_HEADS
ML_CHUNK = 128
DA_HEADS = 8
DA_DH = HALF_W // (2 * DA_HEADS)
DA_DV = 2 * DA_DH
Q_BLOCK = 128
ROPE_BASE = 10000.0
N_EXPERTS = 16
EC_CAPACITY = 2
ML_IN = 4 * HALF_W + 4 * ML_HEADS


def _cparams(*sem, vmem_mib=None):
    limit = V7X_VMEM_LIMIT_BYTES if vmem_mib is None else vmem_mib * 1024 * 1024
    return pltpu.CompilerParams(dimension_semantics=sem, vmem_limit_bytes=limit)


def _dot(a, b):
    return jnp.dot(a, b, preferred_element_type=F32)


def _ada_kernel(c_ref, w_ref, b_ref, o_ref):
    c = c_ref[...]
    a = c * jax.nn.sigmoid(c)
    a_hi = a.astype(BF16)
    a_lo = (a - a_hi.astype(F32)).astype(BF16)
    w = w_ref[...]
    w_hi = w.astype(BF16)
    w_lo = (w - w_hi.astype(F32)).astype(BF16)
    n = a.shape[0]
    r1 = _dot(jnp.concatenate([a_hi, a_lo], axis=0), w_hi)
    r2 = _dot(a_hi, w_lo)
    o_ref[...] = r1[:n] + r1[n:] + r2 + b_ref[...]


def ada_mod(cvec, ada_w, ada_b, tn=512):
    n_lyr, d, n = ada_w.shape
    r = cvec.shape[0]
    return pl.pallas_call(
        _ada_kernel,
        grid=(n_lyr, n // tn),
        in_specs=[pl.BlockSpec((r, d), lambda l, j: (0, 0)),
                  pl.BlockSpec((None, d, tn), lambda l, j: (l, 0, j)),
                  pl.BlockSpec((None, 1, tn), lambda l, j: (l, 0, j))],
        out_specs=pl.BlockSpec((None, r, tn), lambda l, j: (l, 0, j)),
        out_shape=jax.ShapeDtypeStruct((n_lyr, r, n), F32),
        compiler_params=_cparams("parallel", "parallel"),
        name="ada_mod",
    )(cvec, ada_w, ada_b.reshape(n_lyr, 1, n))


def _mod_spec(d, chunk, row_fn, ngrid):
    if ngrid == 2:
        return pl.BlockSpec((None, 1, d), lambda i, j: (row_fn(i), 0, chunk))
    return pl.BlockSpec((None, 1, d), lambda i: (row_fn(i), 0, chunk))


def _norm_mod(x, g, shift, scale):
    y = x * lax.rsqrt(jnp.mean(x * x, axis=-1, keepdims=True) + NORM_EPS) * g
    return y * (1.0 + scale) + shift


def _inproj_kernel(h_ref, g_ref, sh_ref, sc_ref, w_ref, o_ref, a_scr):
    @pl.when(pl.program_id(1) == 0)
    def _():
        rows = min(NORM_ROWS, h_ref.shape[0])

        def body(r, carry):
            sl = pl.ds(pl.multiple_of(r * rows, rows), rows)
            a_scr[sl, :] = _norm_mod(h_ref[sl, :], g_ref[...], sh_ref[...], sc_ref[...]).astype(BF16)
            return carry

        lax.fori_loop(0, h_ref.shape[0] // rows, body, 0)

    o_ref[...] = _dot(a_scr[...], w_ref[...])


def in_proj(h, g, mod, row_fn, w, tm=2048, tn=512):
    t, d = h.shape
    n = w.shape[1]
    tm = min(tm, t)
    if n % tn:
        tn = 256 if n % 256 == 0 else 128
    return pl.pallas_call(
        _inproj_kernel,
        grid=(t // tm, n // tn),
        in_specs=[pl.BlockSpec((tm, d), lambda i, j: (i, 0), pipeline_mode=pl.Buffered(1)),
                  pl.BlockSpec((1, d), lambda i, j: (0, 0)),
                  _mod_spec(d, 0, row_fn, 2), _mod_spec(d, 1, row_fn, 2),
                  pl.BlockSpec((d, tn), lambda i, j: (0, j))],
        out_specs=pl.BlockSpec((tm, tn), lambda i, j: (i, j)),
        out_shape=jax.ShapeDtypeStruct((t, n), F32),
        scratch_shapes=[pltpu.VMEM((tm, d), BF16)],
        compiler_params=_cparams("parallel", "arbitrary", vmem_mib=48),
        name="in_proj",
    )(h, g, mod, mod, w)


def _outproj_kernel(ya_ref, yb_ref, w_ref, h_ref, gate_ref, o_ref):
    ka = ya_ref.shape[1]
    acc = _dot(ya_ref[...], w_ref[:ka, :]) + _dot(yb_ref[...], w_ref[ka:, :])
    o_ref[...] = h_ref[...] + gate_ref[...] * acc


def out_proj(ya, yb, w, h, mod, row_fn, tm=1024, tn=512):
    t, d = h.shape
    ka, kb = ya.shape[1], yb.shape[1]
    tm, tn = min(tm, t), min(tn, d)
    return pl.pallas_call(
        _outproj_kernel,
        grid=(t // tm, d // tn),
        in_specs=[pl.BlockSpec((tm, ka), lambda i, j: (i, 0)),
                  pl.BlockSpec((tm, kb), lambda i, j: (i, 0)),
                  pl.BlockSpec((ka + kb, tn), lambda i, j: (0, j)),
                  pl.BlockSpec((tm, tn), lambda i, j: (i, j)),
                  pl.BlockSpec((None, 1, tn), lambda i, j: (row_fn(i), 0, 2 * (d // tn) + j))],
        out_specs=pl.BlockSpec((tm, tn), lambda i, j: (i, j)),
        out_shape=jax.ShapeDtypeStruct((t, d), F32),
        compiler_params=_cparams("parallel", "parallel", vmem_mib=32),
        name="out_proj",
    )(ya, yb, w, h, mod)


def _moeprep_kernel(h_ref, g_ref, sh_ref, sc_ref, rw_ref, a_ref, lg_ref):
    a = _norm_mod(h_ref[...], g_ref[...], sh_ref[...], sc_ref[...])
    a_ref[...] = a.astype(BF16)
    lg_ref[...] = jnp.dot(a, rw_ref[...], precision=HI, preferred_element_type=F32)


def moe_prep(h, g, mod, row_fn, router_w, tm=512):
    t, d = h.shape
    e = router_w.shape[1]
    tm = min(tm, t)
    return pl.pallas_call(
        _moeprep_kernel,
        grid=(t // tm,),
        in_specs=[pl.BlockSpec((tm, d), lambda i: (i, 0)),
                  pl.BlockSpec((1, d), lambda i: (0, 0)),
                  _mod_spec(d, 3, row_fn, 1), _mod_spec(d, 4, row_fn, 1),
                  pl.BlockSpec((d, e), lambda i: (0, 0))],
        out_specs=[pl.BlockSpec((tm, d), lambda i: (i, 0)), pl.BlockSpec((tm, e), lambda i: (i, 0))],
        out_shape=[jax.ShapeDtypeStruct((t, d), BF16), jax.ShapeDtypeStruct((t, e), F32)],
        compiler_params=_cparams("parallel"),
        name="moe_prep",
    )(h, g, mod, mod, router_w)


def _route_kernel(lg_ref, pos_ref, gate_ref, *, cap):
    lg = lg_ref[...]
    ts, ne = lg.shape
    ex = jnp.exp(lg - jnp.max(lg, axis=-1, keepdims=True))
    aff = ex / jnp.sum(ex, axis=-1, keepdims=True)
    bits = pltpu.bitcast(aff, jnp.int32)

    def bisect(i, thr):
        cand = thr | jnp.left_shift(jnp.int32(1), 30 - i)
        cnt = jnp.sum((bits >= cand).astype(F32), axis=0, keepdims=True)
        return jnp.where(cnt >= cap, cand, thr)

    thr = lax.fori_loop(0, 31, bisect, jnp.zeros((1, ne), jnp.int32))
    gt = bits > thr
    eq = bits == thr
    n_gt = jnp.sum(gt.astype(F32), axis=0, keepdims=True)
    r = lax.broadcasted_iota(jnp.int32, (ts, ts), 0)
    c = lax.broadcasted_iota(jnp.int32, (ts, ts), 1)
    tri = (c < r).astype(BF16)
    eq_rank = _dot(tri, eq.astype(BF16))
    sel = gt | (eq & (eq_rank < cap - n_gt))
    pos = _dot(tri, sel.astype(BF16))
    pos_ref[...] = jnp.where(sel, pos, -1.0)
    gate_ref[...] = jnp.where(sel, aff, 0.0)


def moe_route(logits, n_samples, cap):
    t, e = logits.shape
    ts = t // n_samples
    spec = pl.BlockSpec((ts, e), lambda s: (s, 0))
    return pl.pallas_call(
        functools.partial(_route_kernel, cap=cap),
        grid=(n_samples,),
        in_specs=[spec],
        out_specs=[spec, spec],
        out_shape=[jax.ShapeDtypeStruct((t, e), F32)] * 2,
        compiler_params=_cparams("parallel"),
        name="moe_route",
    )(logits)


def _slot_onehot(pos_col, cap):
    slots = lax.broadcasted_iota(jnp.int32, (1, cap), 1).astype(F32)
    return (pos_col == slots).astype(BF16)


def _gather_kernel(pos_ref, a_ref, x_ref, *, cap):
    pos = pos_ref[...]
    a = a_ref[...]
    for e in range(pos.shape[1]):
        pt = _slot_onehot(pos[:, e:e + 1], cap)
        x = lax.dot_general(pt, a, (((0,), (0,)), ((), ())), preferred_element_type=F32)
        x_ref[e] = x.astype(BF16)


def moe_gather(pos, a, n_samples, cap, tn=1024):
    t, d = a.shape
    e = pos.shape[1]
    ts = t // n_samples
    tn = min(tn, d)
    return pl.pallas_call(
        functools.partial(_gather_kernel, cap=cap),
        grid=(n_samples, d // tn),
        in_specs=[pl.BlockSpec((ts, e), lambda s, j: (s, 0)), pl.BlockSpec((ts, tn), lambda s, j: (s, j))],
        out_specs=pl.BlockSpec((e, cap, tn), lambda s, j: (0, s, j)),
        out_shape=jax.ShapeDtypeStruct((e, n_samples * cap, d), BF16),
        compiler_params=_cparams("parallel", "parallel"),
        name="moe_gather",
    )(pos, a)


def _ffn_kernel(*refs, nx, nff):
    x_refs = refs[:nx]
    w1_ref, w3_ref, w2_ref = refs[nx:nx + 3]
    y_refs = refs[nx + 3:2 * nx + 3]
    hid_refs = refs[2 * nx + 3:]
    j = pl.program_id(1)

    @pl.when(j < nff)
    def _():
        w1 = w1_ref[...].astype(BF16)
        w3 = w3_ref[...].astype(BF16)
        for x_ref, hid_ref in zip(x_refs, hid_refs):
            x = x_ref[...]
            h1 = _dot(x, w1)
            h3 = _dot(x, w3)
            hid_ref[j] = (h1 * jax.nn.sigmoid(h1) * h3).astype(BF16)

    @pl.when(j >= nff)
    def _():
        w2 = w2_ref[...].astype(BF16)
        tf = w2.shape[0] // nff
        for y_ref, hid_ref in zip(y_refs, hid_refs):
            acc = _dot(hid_ref[0], w2[:tf])
            for jj in range(1, nff):
                acc += _dot(hid_ref[jj], w2[jj * tf:(jj + 1) * tf])
            y_ref[...] = acc.astype(y_ref.dtype)


def moe_ffn(xs, w1, w3, w2, layer, tf=256, tn=256):
    _, ne, d, ff = w1.shape
    tf, tn = min(tf, ff), min(tn, d)
    nff, nd = ff // tf, d // tn
    nx = len(xs)
    x_specs = [pl.BlockSpec((None, x.shape[1], d), lambda e, j: (e, 0, 0)) for x in xs]
    up_spec = pl.BlockSpec((None, None, d, tf), lambda e, j: (layer, e, 0, jnp.minimum(j, nff - 1)))
    down_spec = pl.BlockSpec((None, None, ff, tn), lambda e, j: (layer, e, 0, jnp.maximum(j - nff, 0)))
    y_specs = [pl.BlockSpec((None, x.shape[1], tn), lambda e, j: (e, 0, jnp.maximum(j - nff, 0))) for x in xs]
    return pl.pallas_call(
        functools.partial(_ffn_kernel, nx=nx, nff=nff),
        grid=(ne, nff + nd),
        in_specs=x_specs + [up_spec, up_spec, down_spec],
        out_specs=y_specs,
        out_shape=[jax.ShapeDtypeStruct(x.shape, BF16) for x in xs],
        scratch_shapes=[pltpu.VMEM((nff, x.shape[1], tf), BF16) for x in xs],
        compiler_params=_cparams("parallel", "arbitrary"),
        name="moe_ffn",
    )(*xs, w1, w3, w2)


def _combine_kernel(pos_ref, gate_ref, y_ref, h_ref, m_ref, o_ref, pt_scr, *, cap):
    ne = pos_ref.shape[1]

    @pl.when(pl.program_id(1) == 0)
    def _():
        pos = pos_ref[...]
        for e in range(ne):
            pt_scr[e] = _slot_onehot(pos[:, e:e + 1], cap)

    gate = gate_ref[...]
    acc = gate[:, 0:1] * _dot(pt_scr[0], y_ref[0])
    for e in range(1, ne):
        acc += gate[:, e:e + 1] * _dot(pt_scr[e], y_ref[e])
    o_ref[...] = h_ref[...] + m_ref[...] * acc


def moe_combine(pos, gate, y, h, mod, row_fn, n_samples, cap, tn=512):
    t, d = h.shape
    e = pos.shape[1]
    ts = t // n_samples
    tn = min(tn, d)
    return pl.pallas_call(
        functools.partial(_combine_kernel, cap=cap),
        grid=(n_samples, d // tn),
        in_specs=[pl.BlockSpec((ts, e), lambda s, j: (s, 0)),
                  pl.BlockSpec((ts, e), lambda s, j: (s, 0)),
                  pl.BlockSpec((e, cap, tn), lambda s, j: (0, s, j)),
                  pl.BlockSpec((ts, tn), lambda s, j: (s, j)),
                  pl.BlockSpec((None, 1, tn), lambda s, j: (row_fn(s), 0, 5 * (d // tn) + j))],
        out_specs=pl.BlockSpec((ts, tn), lambda s, j: (s, j)),
        out_shape=jax.ShapeDtypeStruct((t, d), F32),
        scratch_shapes=[pltpu.VMEM((e, ts, cap), BF16)],
        compiler_params=_cparams("parallel", "arbitrary"),
        name="moe_combine",
    )(pos, gate, y, h, mod)


def _rms_kernel(x_ref, g_ref, o_ref):
    x = x_ref[...]
    o_ref[...] = x * lax.rsqrt(jnp.mean(x * x, axis=-1, keepdims=True) + NORM_EPS) * g_ref[...]


def final_norm(h, g, tm=512):
    t, d = h.shape
    return pl.pallas_call(
        _rms_kernel,
        grid=(t // tm,),
        in_specs=[pl.BlockSpec((tm, d), lambda i: (i, 0)), pl.BlockSpec((1, d), lambda i: (0, 0))],
        out_specs=pl.BlockSpec((tm, d), lambda i: (i, 0)),
        out_shape=jax.ShapeDtypeStruct((t, d), F32),
        compiler_params=_cparams("parallel"),
        name="final_norm",
    )(h, g)


def tile_rows(row_of_sample, ts, tm):
    per = ts // tm
    return lambda i: row_of_sample(i // per)


def moe_layer(streams, g2, mod, router_w, w1, w3, w2, layer):
    routed = []
    for h, row_of_sample, n_samples in streams:
        ts = h.shape[0] // n_samples
        cap = EC_CAPACITY * ts // N_EXPERTS
        tm = min(512, ts)
        a, logits = moe_prep(h, g2, mod, tile_rows(row_of_sample, ts, tm), router_w, tm=tm)
        pos, gate = moe_route(logits, n_samples, cap)
        routed.append((pos, gate, cap, moe_gather(pos, a, n_samples, cap)))
    ys = moe_ffn([r[3] for r in routed], w1, w3, w2, layer)
    return [moe_combine(pos, gate, y, h, mod, row_of_sample, n_samples, cap)
            for (h, row_of_sample, n_samples), (pos, gate, cap, _), y in zip(streams, routed, ys)]


def _dir_tri(n, d):
    i = lax.broadcasted_iota(jnp.int32, (n, n), 0)
    j = lax.broadcasted_iota(jnp.int32, (n, n), 1)
    return (j - i) * (1 - 2 * d) <= 0


def _gla_chunk(q_ref, k_ref, v_ref, c_ref, w2_ref, gb_ref, o_ref, st_ref, d):
    cs = q_ref.shape[0]
    causal = _dir_tri(cs, d)
    logits = jnp.dot(c_ref[...], w2_ref[...], precision=HI, preferred_element_type=F32) + gb_ref[...]
    logg = jax.nn.log_sigmoid(logits) * (1.0 / GLA_TAU)
    b = jnp.dot(causal.astype(F32), logg, precision=HI, preferred_element_type=F32)
    b_last = jnp.sum(logg, axis=0, keepdims=True)
    q = q_ref[...] * (GLA_DK ** -0.5)
    k = k_ref[...]
    q_dec = (q * jnp.exp(b)).astype(BF16)
    k_dec = (k * jnp.exp(-b)).astype(BF16)
    k_end = (k * jnp.exp(b_last - b)).astype(BF16)
    v = v_ref[...].astype(BF16)
    for h in range(GLA_HEADS):
        ks = slice(h * GLA_DK, (h + 1) * GLA_DK)
        vs = slice(h * GLA_DV, (h + 1) * GLA_DV)
        st = st_ref[h]
        att = lax.dot_general(q_dec[:, ks], k_dec[:, ks], (((1,), (1,)), ((), ())), preferred_element_type=F32)
        att = jnp.where(causal, att, 0.0).astype(BF16)
        o = _dot(att, v[:, vs]) + lax.dot_general(q_dec[:, ks], st.astype(BF16), (((1,), (1,)), ((), ())),
                                                  preferred_element_type=F32)
        o_ref[:, vs] = o
        upd = lax.dot_general(v[:, vs], k_end[:, ks], (((0,), (0,)), ((), ())), preferred_element_type=F32)
        st_ref[h] = st * jnp.exp(b_last[:, ks]) + upd


def _gla_kernel(qc_ref, kc_ref, vc_ref, cc_ref, ql_ref, kl_ref, vl_ref, cl_ref, w2_ref, gb_ref,
                oc_ref, ol_ref, st_ref, *, ncc):
    d = pl.program_id(1)
    c = pl.program_id(2)

    @pl.when(c == 0)
    def _():
        st_ref[...] = jnp.zeros_like(st_ref)

    @pl.when(c < ncc)
    def _():
        for s in range(qc_ref.shape[0]):
            _gla_chunk(qc_ref.at[s], kc_ref.at[s], vc_ref.at[s], cc_ref.at[s], w2_ref, gb_ref, oc_ref.at[s],
                       st_ref.at[s], d)

    @pl.when(c >= ncc)
    def _():
        for s in range(ql_ref.shape[0]):
            _gla_chunk(ql_ref.at[s], kl_ref.at[s], vl_ref.at[s], cl_ref.at[s], w2_ref, gb_ref, ol_ref.at[s],
                       st_ref.at[s], d)


def _scan_chunk_maps(n_samples_chunks_ctx, n_samples_chunks_lat):
    ncc, nlc = n_samples_chunks_ctx, n_samples_chunks_lat

    def ctx_blk(b, d, c):
        i = jnp.minimum(c, ncc - 1)
        return b * ncc + jnp.where(d == 0, i, ncc - 1 - i)

    def lat_blk(b, d, c):
        i = jnp.maximum(c - ncc, 0)
        return b * nlc + jnp.where(d == 0, i, nlc - 1 - i)

    return ctx_blk, lat_blk


def gla_scan_call(p_ctx, p_lat, w2_full, gate_b, n_samples, col0, col_codes):
    cs = GLA_CHUNK
    hk, hv = GLA_HEADS * GLA_DK, GLA_HEADS * GLA_DV
    ncc, nlc = p_ctx.shape[0] // n_samples // cs, p_lat.shape[0] // n_samples // cs
    ctx_blk, lat_blk = _scan_chunk_maps(ncc, nlc)
    qb, kb, vb = col0 // hk, col0 // hk + 1, (col0 + 2 * hk) // hv
    cb = col_codes // LANES
    pair = SCAN_PAIR if n_samples % SCAN_PAIR == 0 else 1
    p_ctx = p_ctx.reshape(pair, p_ctx.shape[0] // pair, p_ctx.shape[1])
    p_lat = p_lat.reshape(pair, p_lat.shape[0] // pair, p_lat.shape[1])

    def specs(blk):
        return [pl.BlockSpec((pair, cs, hk), lambda b, d, c: (0, blk(b, d, c), qb)),
                pl.BlockSpec((pair, cs, hk), lambda b, d, c: (0, blk(b, d, c), kb)),
                pl.BlockSpec((pair, cs, hv), lambda b, d, c: (0, blk(b, d, c), vb)),
                pl.BlockSpec((pair, cs, 128), lambda b, d, c: (0, blk(b, d, c), cb))]

    o_ctx, o_lat = pl.pallas_call(
        functools.partial(_gla_kernel, ncc=ncc),
        grid=(n_samples // pair, 2, ncc + nlc),
        in_specs=specs(ctx_blk) + specs(lat_blk) + [
            pl.BlockSpec((None, 128, hk), lambda b, d, c: (d, 0, 0)),
            pl.BlockSpec((None, 1, hk), lambda b, d, c: (d, 0, 0))],
        out_specs=[pl.BlockSpec((None, pair, cs, hv), lambda b, d, c: (d, 0, ctx_blk(b, d, c), 0)),
                   pl.BlockSpec((None, pair, cs, hv), lambda b, d, c: (d, 0, lat_blk(b, d, c), 0))],
        out_shape=[jax.ShapeDtypeStruct((2,) + p_ctx.shape[:2] + (hv,), F32),
                   jax.ShapeDtypeStruct((2,) + p_lat.shape[:2] + (hv,), F32)],
        scratch_shapes=[pltpu.VMEM((pair, GLA_HEADS, GLA_DV, GLA_DK), F32)],
        compiler_params=_cparams("parallel", "arbitrary", "arbitrary", vmem_mib=32),
        name="gla_scan",
    )(p_ctx, p_ctx, p_ctx, p_ctx, p_lat, p_lat, p_lat, p_lat, w2_full, gate_b)
    return o_ctx.reshape(2, -1, hv), o_lat.reshape(2, -1, hv)


def _headnorm_gate_kernel(o_ref, r_ref, g_ref, y_ref, *, n_heads, act):
    o = o_ref[0] + o_ref[1]
    r = r_ref[...]
    gate = r * jax.nn.sigmoid(r) if act == "silu" else jax.nn.sigmoid(r)
    dh = o.shape[1] // n_heads
    for h in range(n_heads):
        s = slice(h * dh, (h + 1) * dh)
        oh = o[:, s]
        yh = oh * lax.rsqrt(jnp.mean(oh * oh, axis=-1, keepdims=True) + NORM_EPS) * g_ref[...]
        y_ref[:, s] = (yh * gate[:, s]).astype(y_ref.dtype)


def headnorm_gate(o2, p, r_col, norm_g, n_heads, act, tm=512):
    _, t, w = o2.shape
    tm = min(tm, t)
    return pl.pallas_call(
        functools.partial(_headnorm_gate_kernel, n_heads=n_heads, act=act),
        grid=(t // tm,),
        in_specs=[pl.BlockSpec((2, tm, w), lambda i: (0, i, 0)),
                  pl.BlockSpec((tm, w), lambda i: (i, r_col // w)),
                  pl.BlockSpec((1, w // n_heads), lambda i: (0, 0))],
        out_specs=pl.BlockSpec((tm, w), lambda i: (i, 0)),
        out_shape=jax.ShapeDtypeStruct((t, w), BF16),
        compiler_params=_cparams("parallel", vmem_mib=32),
        name="headnorm_gate",
    )(o2, p, norm_g.reshape(1, -1))


def gla_mixer(p_ctx, p_lat, gate_w2, gate_b, norm_g, n_samples, col0):
    hk, hv = GLA_HEADS * GLA_DK, GLA_HEADS * GLA_DV
    w2_full = jnp.zeros((2, 128, hk), F32)
    for d in range(2):
        w2_full = w2_full.at[d, d * GLA_RANK:(d + 1) * GLA_RANK].set(gate_w2[d])
    r_col = col0 + 2 * hk + hv
    o_ctx, o_lat = gla_scan_call(p_ctx, p_lat, w2_full, gate_b.reshape(2, 1, hk), n_samples, col0, r_col + hv)
    return (headnorm_gate(o_ctx, p_ctx, r_col, norm_g, GLA_HEADS, "silu"),
            headnorm_gate(o_lat, p_lat, r_col, norm_g, GLA_HEADS, "silu"))


ROPE_NF = DA_DH // 4
LANES = 128


def _rope_table_kernel(cos_ref, sin_ref):
    n = cos_ref.shape[0]
    t = lax.broadcasted_iota(jnp.int32, (n, LANES), 0)
    j = lax.broadcasted_iota(jnp.int32, (n, LANES), 1)
    dd = j % DA_DH
    is_col = (dd // (2 * ROPE_NF)) == 1
    is_xb = ((dd // ROPE_NF) % 2) == 1
    f = (dd % ROPE_NF).astype(F32)
    inv = jnp.exp(f * (-math.log(ROPE_BASE) / ROPE_NF))
    pos = jnp.where(is_col, t % GRID_W, t // GRID_W).astype(F32)
    ang = pos * inv
    cos_ref[...] = jnp.cos(ang)
    sin_ref[...] = jnp.where(is_xb, 1.0, -1.0) * jnp.sin(ang)


def rope_tables(n):
    out = jax.ShapeDtypeStruct((n, LANES), F32)
    return pl.pallas_call(_rope_table_kernel, out_shape=[out, out], name="rope_tables")()


def _rope(x, cos, sin_signed):
    lane = lax.broadcasted_iota(jnp.int32, x.shape, 1)
    is_xb = ((lane // ROPE_NF) % 2) == 1
    partner = jnp.where(is_xb, pltpu.roll(x, ROPE_NF, 1), pltpu.roll(x, LANES - ROPE_NF, 1))
    return x * cos + partner * sin_signed


def _nt(a, b):
    return lax.dot_general(a, b, (((1,), (1,)), ((), ())), preferred_element_type=F32)


def _diffattn_kernel(q_ref, kl_ref, vl_ref, kc_ref, vc_ref, cq_ref, sq_ref, ck_ref, sk_ref, lam_ref, g_ref, o_ref,
                     kl_scr, vl_scr, kc_scr, vc_scr, *, lam_init):
    @pl.when(pl.program_id(2) == 0)
    def _():
        kl_scr[...] = _rope(kl_ref[...], ck_ref[...], sk_ref[...]).astype(BF16)
        kc_scr[...] = kc_ref[...].astype(BF16)
        vl_scr[:, :LANES] = vl_ref[...].astype(BF16)
        vl_scr[:, LANES:] = jnp.ones(vl_ref.shape, BF16)
        vc_scr[:, :LANES] = vc_ref[...].astype(BF16)
        vc_scr[:, LANES:] = jnp.ones(vc_ref.shape, BF16)

    lv = lam_ref[...]
    lam = (jnp.exp(jnp.sum(lv[0:1] * lv[1:2], keepdims=True)) - jnp.exp(jnp.sum(lv[2:3] * lv[3:4], keepdims=True))
           + lam_init)
    q = _rope(q_ref[...], cq_ref[...], sq_ref[...]) * (DA_DH ** -0.5)
    first = lax.broadcasted_iota(jnp.int32, q.shape, 1) < DA_DH
    tq = q.shape[0]
    q2 = jnp.concatenate([jnp.where(first, q, 0.0), jnp.where(first, 0.0, q)], axis=0).astype(BF16)
    s_l = _nt(q2, kl_scr[...])
    s_c = _nt(q2, kc_scr[...])
    m = jnp.maximum(jnp.max(s_l, axis=-1, keepdims=True), jnp.max(s_c, axis=-1, keepdims=True))
    acc = _dot(jnp.exp(s_l - m).astype(BF16), vl_scr[...]) + _dot(jnp.exp(s_c - m).astype(BF16), vc_scr[...])
    attn = acc[:, :LANES] / acc[:, LANES:]
    o = attn[:tq] - lam * attn[tq:]
    o = o * lax.rsqrt(jnp.mean(o * o, axis=-1, keepdims=True) + NORM_EPS) * g_ref[...]
    o_ref[...] = (o * (1.0 - lam_init)).astype(o_ref.dtype)


def diff_attention_call(p_ctx, p_lat, n_samples, col_q, lam_vecs, norm_g, lam_init, tq=512):
    lt, lc = p_lat.shape[0] // n_samples, p_ctx.shape[0] // n_samples
    tq = min(tq, lt)
    nq = lt // tq
    w = DA_HEADS * DA_DV
    qb, kb, vb = col_q // LANES, (col_q + w) // LANES, (col_q + 2 * w) // LANES
    cos, sin = rope_tables(lt)
    return pl.pallas_call(
        functools.partial(_diffattn_kernel, lam_init=lam_init),
        grid=(n_samples, DA_HEADS, nq),
        in_specs=[pl.BlockSpec((tq, LANES), lambda b, h, i: (b * nq + i, qb + h)),
                  pl.BlockSpec((lt, LANES), lambda b, h, i: (b, kb + h)),
                  pl.BlockSpec((lt, LANES), lambda b, h, i: (b, vb + h)),
                  pl.BlockSpec((lc, LANES), lambda b, h, i: (b, kb + h)),
                  pl.BlockSpec((lc, LANES), lambda b, h, i: (b, vb + h)),
                  pl.BlockSpec((tq, LANES), lambda b, h, i: (i, 0)),
                  pl.BlockSpec((tq, LANES), lambda b, h, i: (i, 0)),
                  pl.BlockSpec((lt, LANES), lambda b, h, i: (0, 0)),
                  pl.BlockSpec((lt, LANES), lambda b, h, i: (0, 0)),
                  pl.BlockSpec(lam_vecs.shape, lambda b, h, i: (0, 0)),
                  pl.BlockSpec((1, DA_DV), lambda b, h, i: (0, 0))],
        out_specs=pl.BlockSpec((tq, LANES), lambda b, h, i: (b * nq + i, h)),
        out_shape=jax.ShapeDtypeStruct((p_lat.shape[0], w), BF16),
        scratch_shapes=[pltpu.VMEM((lt, LANES), BF16), pltpu.VMEM((lt, 2 * LANES), BF16),
                        pltpu.VMEM((lc, LANES), BF16), pltpu.VMEM((lc, 2 * LANES), BF16)],
        compiler_params=_cparams("parallel", "parallel", "arbitrary", vmem_mib=48),
        name="diff_attention",
    )(p_lat, p_lat, p_lat, p_ctx, p_ctx, cos, sin, cos, sin, lam_vecs, norm_g.reshape(1, DA_DV))


def _convsilu_kernel(u_ref, w_ref, b_ref, s_ref, o_ref, *, silu):
    u = u_ref[...]
    n = u.shape[0]
    row = lax.broadcasted_iota(jnp.int32, u.shape, 0)
    prev = jnp.where(row == 0, 0.0, pltpu.roll(u, 1, 0))
    nxt = jnp.where(row == n - 1, 0.0, pltpu.roll(u, n - 1, 0))
    w = w_ref[...]
    y = w[0:1] * prev + w[1:2] * u + w[2:3] * nxt + b_ref[...]
    if silu:
        y = y * jax.nn.sigmoid(y)
    o_ref[...] = y * s_ref[...]


def conv_silu(p, n_samples, width, conv_w, conv_b, col_scale, tn=512, silu=True):
    ls = p.shape[0] // n_samples
    return pl.pallas_call(
        functools.partial(_convsilu_kernel, silu=silu),
        grid=(n_samples, width // tn),
        in_specs=[pl.BlockSpec((ls, tn), lambda s, j: (s, j)),
                  pl.BlockSpec((conv_w.shape[0], tn), lambda s, j: (0, j)),
                  pl.BlockSpec((1, tn), lambda s, j: (0, j)),
                  pl.BlockSpec((1, tn), lambda s, j: (0, j))],
        out_specs=pl.BlockSpec((ls, tn), lambda s, j: (s, j)),
        out_shape=jax.ShapeDtypeStruct((p.shape[0], width), F32),
        compiler_params=_cparams("parallel", "parallel", vmem_mib=32),
        name="conv_silu",
    )(p, conv_w, conv_b.reshape(1, -1), col_scale.reshape(1, -1))


def _mlstm_chunk(q_ref, k_ref, v_ref, g_ref, gb_ref, o_ref, c_ref, n_ref, m_ref, d, emit):
    cs = q_ref.shape[0]
    causal = _dir_tri(cs, d)
    g = g_ref[...] + gb_ref[...]
    lane = lax.broadcasted_iota(jnp.int32, g.shape, 1)
    gl = jnp.where((lane // ML_HEADS) % 2 == 1, jax.nn.log_sigmoid(g), g)
    bcum = jnp.dot(causal.astype(F32), gl, precision=HI, preferred_element_type=F32)
    gl_t = gl.T
    bcum_t = bcum.T
    q = q_ref[...].astype(BF16)
    k = k_ref[...]
    kb = k.astype(BF16)
    v = v_ref[...]

    def pick_col(a, idx):
        return jnp.where(d == 0, a[:, idx:idx + 1], a[:, 2 * ML_HEADS + idx:2 * ML_HEADS + idx + 1])

    def pick_row(a, idx):
        return jnp.where(d == 0, a[idx:idx + 1, :], a[2 * ML_HEADS + idx:2 * ML_HEADS + idx + 1, :])

    for h in range(ML_HEADS):
        hs = slice(h * ML_DH, (h + 1) * ML_DH)
        ic_c, ic_r = pick_col(gl, h), pick_row(gl_t, h)
        fc_c = pick_col(gl, ML_HEADS + h)
        b_c, b_r = pick_col(bcum, ML_HEADS + h), pick_row(bcum_t, ML_HEADS + h)
        b_last = jnp.sum(fc_c, axis=0, keepdims=True)
        m = m_ref[h][:, 0:1]
        cm = c_ref[h]
        nv = n_ref[h]
        w_end_c = b_last - b_c + ic_c
        w_end_r = b_last - b_r + ic_r
        m_new = jnp.maximum(b_last + m, jnp.max(w_end_r, axis=-1, keepdims=True))
        keep = jnp.exp(b_last + m - m_new)
        w_c = jnp.exp(w_end_c - m_new)
        vw = (v[:, hs] * w_c).astype(BF16)
        c_ref[h] = keep * cm + lax.dot_general(vw, kb[:, hs], (((0,), (0,)), ((), ())), preferred_element_type=F32)
        n_ref[h] = keep * nv + jnp.sum(w_c * k[:, hs], axis=0, keepdims=True)
        m_ref[h] = jnp.broadcast_to(m_new, m_ref.shape[1:])
        if emit:
            a_c = b_c + m
            dlog = jnp.where(causal, b_c - b_r + ic_r, -jnp.inf)
            m_t = jnp.maximum(a_c, jnp.max(dlog, axis=-1, keepdims=True))
            sc = _nt(q[:, hs], kb[:, hs]) * jnp.exp(dlog - m_t)
            aw = jnp.exp(a_c - m_t)
            num = _dot(sc.astype(BF16), v[:, hs].astype(BF16)) + aw * _nt(q[:, hs], cm.astype(BF16))
            qf = q_ref[:, hs]
            den = jnp.sum(sc, axis=-1, keepdims=True) + aw * jnp.sum(qf * nv, axis=-1, keepdims=True)
            o_ref[:, hs] = num / jnp.maximum(jnp.abs(den), jnp.exp(-m_t))


def _mlstm_kernel(qc_ref, kc_ref, vc_ref, gc_ref, ql_ref, kl_ref, vl_ref, gl_ref, gb_ref,
                  ol_ref, c_ref, n_ref, m_ref, *, ncc):
    d = pl.program_id(1)
    c = pl.program_id(2)

    @pl.when(c == 0)
    def _():
        c_ref[...] = jnp.zeros_like(c_ref)
        n_ref[...] = jnp.zeros_like(n_ref)
        m_ref[...] = jnp.zeros_like(m_ref)

    @pl.when(c < ncc)
    def _():
        for s in range(qc_ref.shape[0]):
            _mlstm_chunk(qc_ref.at[s], kc_ref.at[s], vc_ref.at[s], gc_ref.at[s], gb_ref, None,
                         c_ref.at[s], n_ref.at[s], m_ref.at[s], d, False)

    @pl.when(c >= ncc)
    def _():
        for s in range(ql_ref.shape[0]):
            _mlstm_chunk(ql_ref.at[s], kl_ref.at[s], vl_ref.at[s], gl_ref.at[s], gb_ref, ol_ref.at[s],
                         c_ref.at[s], n_ref.at[s], m_ref.at[s], d, True)


def mlstm_scan_call(qk_ctx, qk_lat, p_ctx, p_lat, gate_bias, n_samples, col_v, col_g):
    cs = ML_CHUNK
    w = ML_HEADS * ML_DH
    ncc, nlc = p_ctx.shape[0] // n_samples // cs, p_lat.shape[0] // n_samples // cs
    ctx_blk, lat_blk = _scan_chunk_maps(ncc, nlc)
    vb = col_v // w
    gcol = col_g // LANES
    pair = SCAN_PAIR if n_samples % SCAN_PAIR == 0 else 1
    halves = lambda a: a.reshape(pair, a.shape[0] // pair, a.shape[1])
    qk_ctx, qk_lat, p_ctx, p_lat = halves(qk_ctx), halves(qk_lat), halves(p_ctx), halves(p_lat)

    def specs(blk):
        return [pl.BlockSpec((pair, cs, w), lambda b, d, c: (0, blk(b, d, c), 0)),
                pl.BlockSpec((pair, cs, w), lambda b, d, c: (0, blk(b, d, c), 1)),
                pl.BlockSpec((pair, cs, w), lambda b, d, c: (0, blk(b, d, c), vb)),
                pl.BlockSpec((pair, cs, LANES), lambda b, d, c: (0, blk(b, d, c), gcol))]

    out = pl.pallas_call(
        functools.partial(_mlstm_kernel, ncc=ncc),
        grid=(n_samples // pair, 2, ncc + nlc),
        in_specs=specs(ctx_blk) + specs(lat_blk) + [pl.BlockSpec((1, LANES), lambda b, d, c: (0, 0))],
        out_specs=pl.BlockSpec((None, pair, cs, w), lambda b, d, c: (d, 0, lat_blk(b, d, c), 0)),
        out_shape=jax.ShapeDtypeStruct((2,) + p_lat.shape[:2] + (w,), F32),
        scratch_shapes=[pltpu.VMEM((pair, ML_HEADS, ML_DH, ML_DH), F32), pltpu.VMEM((pair, ML_HEADS, 1, ML_DH), F32),
                        pltpu.VMEM((pair, ML_HEADS, 1, LANES), F32)],
        compiler_params=_cparams("parallel", "arbitrary", "arbitrary", vmem_mib=32),
        name="mlstm_scan",
    )(qk_ctx, qk_ctx, p_ctx, p_ctx, qk_lat, qk_lat, p_lat, p_lat, gate_bias)
    return out.reshape(2, -1, w)


def mlstm_mixer(p_ctx, p_lat, conv_w, conv_b, igate_b, fgate_b, norm_g, n_samples, col_g):
    w = ML_HEADS * ML_DH
    scale = jnp.concatenate([jnp.ones((w,), F32), jnp.full((w,), ML_DH ** -0.5, F32)])
    qk_ctx = conv_silu(p_ctx, n_samples, 2 * w, conv_w, conv_b, scale)
    qk_lat = conv_silu(p_lat, n_samples, 2 * w, conv_w, conv_b, scale)
    bias = jnp.concatenate([jnp.stack([igate_b[d], fgate_b[d]]).reshape(-1) for d in range(2)])
    gate_bias = jnp.zeros((1, LANES), F32).at[0, :bias.shape[0]].set(bias)
    o2 = mlstm_scan_call(qk_ctx, qk_lat, p_ctx, p_lat, gate_bias, n_samples, 2 * w, col_g)
    return headnorm_gate(o2, p_lat, 3 * w, norm_g, ML_HEADS, "sigmoid")


HY_FEAT_PAD = 128
DFT_FREQ_PAD = 128


def _hyfilt_kernel(w1_ref, b1_ref, fr_ref, w2_ref, b2_ref, w3_ref, o_ref, *, seq):
    tn = o_ref.shape[1]
    pos = lax.broadcasted_iota(jnp.int32, (seq, HY_FEAT_PAD), 0).astype(F32)
    lane = lax.broadcasted_iota(jnp.int32, (seq, HY_FEAT_PAD), 1)
    t = pos / max(seq - 1, 1)
    band = 1e-4 + ((lane - 1) % HY_BANDS).astype(F32) * ((HY_BANDS - 1 - 1e-4) / (HY_BANDS - 1))
    ang = (2.0 * math.pi / seq) * pos * band
    feats = jnp.where(lane == 0, t, jnp.where(lane <= HY_BANDS, jnp.cos(ang),
                                              jnp.where(lane <= 2 * HY_BANDS, -jnp.sin(ang), 0.0)))
    fr = fr_ref[...]
    h = jnp.sin(fr[0:1] * (jnp.dot(feats, w1_ref[...], precision=HI, preferred_element_type=F32) + b1_ref[...]))
    h = jnp.sin(fr[1:2] * (jnp.dot(h, w2_ref[...], precision=HI, preferred_element_type=F32) + b2_ref[...]))
    h = jnp.dot(h, w3_ref[...], precision=HI, preferred_element_type=F32)
    col = pl.program_id(0) * tn + lax.broadcasted_iota(jnp.int32, (1, tn), 1)
    chan = col % HY_W
    lo, hi = math.log(HY_DECAY_TARGET) / HY_SLOW_DECAY, math.log(HY_DECAY_TARGET) / HY_FAST_DECAY
    delta = jnp.abs(lo + chan.astype(F32) * ((hi - lo) / (HY_W - 1)))
    filt = h * jnp.exp(-t[:, 0:1] * delta)
    is_bwd = (col // HY_W) % 2 == 1
    o_ref[...] = jnp.where(is_bwd & (pos[:, 0:1] == 0.0), 0.0, filt)


def hyena_filters_call(seq, w1, b1, freq, w2, b2, w3, tn=512):
    nf = w1.shape[1]
    n = w3.shape[1]
    w1p = jnp.zeros((HY_FEAT_PAD, nf), F32).at[:w1.shape[0]].set(w1)
    full = lambda shape: pl.BlockSpec(shape, lambda j: (0, 0))
    return pl.pallas_call(
        functools.partial(_hyfilt_kernel, seq=seq),
        grid=(n // tn,),
        in_specs=[full((HY_FEAT_PAD, nf)), full((1, nf)), full((2, nf)), full((nf, nf)), full((1, nf)),
                  pl.BlockSpec((nf, tn), lambda j: (0, j))],
        out_specs=pl.BlockSpec((seq, tn), lambda j: (0, j)),
        out_shape=jax.ShapeDtypeStruct((seq, n), F32),
        compiler_params=_cparams("parallel", vmem_mib=32),
        name="hyena_filters",
    )(w1p, b1.reshape(1, nf), freq, w2, b2.reshape(1, nf), w3)


def _dft_trig(f, s, n, minus_sin):
    k = (f * s + jnp.where(minus_sin, n // 4, 0)) % n
    return jnp.cos(k.astype(F32) * (2.0 * math.pi / n))


def _spectrum_slot(idx, nfreq):
    half = nfreq // 2
    within = idx % nfreq
    return (idx // nfreq) * half + within % half, within >= half


def _dft_fwd_kernel(o_ref, *, seq, nfreq):
    tr = o_ref.shape[0]
    r = pl.program_id(0) * tr + lax.broadcasted_iota(jnp.int32, (tr, seq), 0)
    s = lax.broadcasted_iota(jnp.int32, (tr, seq), 1)
    f, is_im = _spectrum_slot(r, nfreq)
    o_ref[...] = jnp.where(f <= seq, _dft_trig(f, s, 2 * seq, is_im), 0.0).astype(o_ref.dtype)


def _dft_inv_kernel(o_ref, *, seq, nfreq):
    tc = o_ref.shape[1]
    c = pl.program_id(0) * tc + lax.broadcasted_iota(jnp.int32, (seq, tc), 1)
    t = lax.broadcasted_iota(jnp.int32, (seq, tc), 0)
    f, is_im = _spectrum_slot(c, nfreq)
    wf = jnp.where((f == 0) | (f == seq), 1.0, jnp.where(f < seq, 2.0, 0.0)) * (0.5 / seq)
    o_ref[...] = (wf * _dft_trig(f, t, 2 * seq, is_im)).astype(o_ref.dtype)


def dft_tables(seq):
    nfreq = seq + DFT_FREQ_PAD
    fwd = pl.pallas_call(
        functools.partial(_dft_fwd_kernel, seq=seq, nfreq=nfreq), grid=(2 * nfreq // 128,),
        out_specs=pl.BlockSpec((128, seq), lambda i: (i, 0)),
        out_shape=jax.ShapeDtypeStruct((2 * nfreq, seq), BF16), compiler_params=_cparams("parallel", vmem_mib=32),
        name="dft_fwd_table")()
    inv = pl.pallas_call(
        functools.partial(_dft_inv_kernel, seq=seq, nfreq=nfreq), grid=(2 * nfreq // 128,),
        out_specs=pl.BlockSpec((seq, 128), lambda j: (0, j)),
        out_shape=jax.ShapeDtypeStruct((seq, 2 * nfreq), BF16), compiler_params=_cparams("parallel", vmem_mib=32),
        name="dft_inv_table")()
    return fwd, inv


def _dft_apply_kernel(a_ref, x_ref, o_ref, xb_scr):
    @pl.when(pl.program_id(2) == 0)
    def _():
        xb_scr[...] = x_ref[...].astype(BF16)

    o_ref[...] = _dot(a_ref[...], xb_scr[...])


def _dft_mul_kernel(a_ref, x_ref, hf_ref, hb_ref, o_ref, xb_scr):
    @pl.when(pl.program_id(2) == 0)
    def _():
        xb_scr[...] = x_ref[...].astype(BF16)

    u = _dot(a_ref[...], xb_scr[...])
    half = u.shape[0] // 2
    ure, uim = u[:half], u[half:]
    hre = hf_ref[0, :half] + hb_ref[0, :half]
    him = hf_ref[0, half:] - hb_ref[0, half:]
    o_ref[:half] = (ure * hre - uim * him).astype(o_ref.dtype)
    o_ref[half:] = (ure * him + uim * hre).astype(o_ref.dtype)


def dft_apply(table, x, n_samples, col0, width, hspec=None, order=0, tn=512):
    m, ls = table.shape
    tm = m // 2
    if hspec is not None:
        tn = 256
    tn = min(tn, width)
    in_specs = [pl.BlockSpec((tm, ls), lambda s, j, i: (i, 0)),
                pl.BlockSpec((ls, tn), lambda s, j, i: (s, col0 // tn + j))]
    args = (table, x)
    if hspec is not None:
        hcol = order * 2 * width // tn
        in_specs += [pl.BlockSpec((1, tm, tn), lambda s, j, i: (0, i, hcol + j)),
                     pl.BlockSpec((1, tm, tn), lambda s, j, i: (0, i, hcol + width // tn + j))]
        args += (hspec, hspec)
    return pl.pallas_call(
        _dft_apply_kernel if hspec is None else _dft_mul_kernel,
        grid=(n_samples, width // tn, m // tm),
        in_specs=in_specs,
        out_specs=pl.BlockSpec((None, tm, tn), lambda s, j, i: (s, i, j)),
        out_shape=jax.ShapeDtypeStruct((n_samples, m, width), F32 if hspec is None else BF16),
        scratch_shapes=[pltpu.VMEM((ls, tn), BF16)],
        compiler_params=_cparams("parallel", "parallel", "arbitrary", vmem_mib=48),
        name="dft_apply" if hspec is None else "dft_mul",
    )(*args)


def _conv_back_kernel(a_ref, y_ref, gate_ref, prev_ref, skip_ref, o_ref):
    y = _dot(a_ref[...], y_ref[...])
    o_ref[...] = (gate_ref[...] * (y + skip_ref[...] * prev_ref[...])).astype(o_ref.dtype)


def conv_back(inv_table, y, gate, gate_col, prev, prev_col, skip, out_dtype, tm=1024, tn=512):
    ns, m2, c = y.shape
    ls = inv_table.shape[0]
    tm = min(tm, ls)
    return pl.pallas_call(
        _conv_back_kernel,
        grid=(ns, c // tn, ls // tm),
        in_specs=[pl.BlockSpec((tm, m2), lambda s, j, i: (i, 0)),
                  pl.BlockSpec((None, m2, tn), lambda s, j, i: (s, 0, j)),
                  pl.BlockSpec((tm, tn), lambda s, j, i: (s * (ls // tm) + i, gate_col // tn + j)),
                  pl.BlockSpec((tm, tn), lambda s, j, i: (s * (ls // tm) + i, prev_col // tn + j)),
                  pl.BlockSpec((1, tn), lambda s, j, i: (0, j))],
        out_specs=pl.BlockSpec((tm, tn), lambda s, j, i: (s * (ls // tm) + i, j)),
        out_shape=jax.ShapeDtypeStruct((ns * ls, c), out_dtype),
        compiler_params=_cparams("parallel", "parallel", "parallel", vmem_mib=48),
        name="conv_back",
    )(inv_table, y, gate, prev, skip)


def hyena_mixer(p, n_samples, conv_w, conv_b, filt_params, skip):
    ls = p.shape[0] // n_samples
    c = HY_W
    fwd, inv = dft_tables(ls)
    filt = hyena_filters_call(ls, *filt_params)
    hspec = dft_apply(fwd, filt, 1, 0, filt.shape[1])
    xc = conv_silu(p, n_samples, 3 * c, conv_w, conv_b, jnp.ones((3 * c,), F32), silu=False)
    v, v_col = xc, 2 * c
    for order in range(HY_ORDER):
        y = dft_apply(fwd, v, n_samples, v_col, c, hspec=hspec, order=order)
        last = order == HY_ORDER - 1
        v = conv_back(inv, y, xc, order * c, v, v_col, skip[order].reshape(1, c), BF16 if last else F32)
        v_col = 0
    return v


def _pad_cols(w, mult=128):
    n = w.shape[-1]
    return jnp.pad(w, ((0, 0), (0, (-n) % mult)))


def kernel(x, c, ctx, c_ctx, ada_w, ada_b, norm1_g, norm2_g, out_w, router_w, moe_w1, moe_w3, moe_w2, final_g, ev_in_w, hy_conv_w, hy_conv_b, hy_pos_w1, hy_pos_b1, hy_sin_freq, hy_pos_w2, hy_pos_b2, hy_pos_w3, hy_bias, gla_gate_w2, gla_gate_b, gla_norm_g, od_in_w, ml_conv_w, ml_conv_b, ml_igate_b, ml_fgate_b, ml_norm_g, da_lambda, da_norm_g):
    B, L, D = x.shape
    Lc = ctx.shape[1]
    depth = ada_w.shape[0]
    cvec = jnp.zeros((N_MOD_ROWS, D), F32).at[:B].set(c).at[B].set(c_ctx)
    mods = ada_mod(cvec, ada_w, ada_b)
    lat_row = lambda s: s
    ctx_row = lambda s: s * 0 + B
    h_x = x.reshape(B * L, D)
    h_c = ctx.reshape(B * Lc, D)
    tm_lat = min(1024, L)
    for l in range(depth):
        last = l == depth - 1
        mod = mods[l].reshape(N_MOD_ROWS, 1, 6 * D)
        g1 = norm1_g[l].reshape(1, D)
        if l % 2 == 0:
            in_w = ev_in_w[l // 2]
        else:
            od = od_in_w[l // 2]
            ml_w = 4 * HALF_W
            in_w = jnp.concatenate([od[:, :ml_w], od[:, ML_IN:], od[:, ml_w:ML_IN]], axis=1)
        w_in = _pad_cols(in_w, 512).astype(BF16)
        w_out = out_w[l].astype(BF16)
        p_lat2 = in_proj(h_x, g1, mod, lat_row, w_in, tm=L)
        p_ctx2 = in_proj(h_c, g1, mod, ctx_row, w_in, tm=B * Lc)
        if l % 2 == 0:
            e = l // 2
            hy_n = 3 * HY_W
            filt_params = (hy_pos_w1[e], hy_pos_b1[e], hy_sin_freq[e], hy_pos_w2[e], hy_pos_b2[e], hy_pos_w3[e])
            ya_l = hyena_mixer(p_lat2, B, hy_conv_w[e], hy_conv_b[e], filt_params, hy_bias[e])
            ya_c = hyena_mixer(p_ctx2, B, hy_conv_w[e], hy_conv_b[e], filt_params, hy_bias[e])
            yb_c, yb_l = gla_mixer(p_ctx2, p_lat2, gla_gate_w2[e], gla_gate_b[e], gla_norm_g[e], B, hy_n)
        else:
            o = l // 2
            lam_init = 0.8 - 0.6 * math.exp(-0.3 * l)
            ya_l = mlstm_mixer(p_ctx2, p_lat2, ml_conv_w[o], ml_conv_b[o], ml_igate_b[o], ml_fgate_b[o], ml_norm_g[o],
                               B, ml_w + 3 * HALF_W)
            yb_l = diff_attention_call(p_ctx2, p_lat2, B, ml_w, da_lambda[o], da_norm_g[o], lam_init)
        h_x = out_proj(ya_l, yb_l, w_out, h_x, mod, tile_rows(lat_row, L, tm_lat), tm=tm_lat)
        streams = [(h_x, lat_row, B)]
        if not last:
            h_c = out_proj(ya_c, yb_c, w_out, h_c, mod, tile_rows(ctx_row, Lc, Lc), tm=Lc)
            streams.append((h_c, ctx_row, B))
        new = moe_layer(streams, norm2_g[l].reshape(1, D), mod, router_w[l], moe_w1, moe_w3, moe_w2, l)
        h_x = new[0]
        if not last:
            h_c = new[1]
    return final_norm(h_x, final_g.reshape(1, D)).reshape(B, L, D)
```

```python
import functools
import math

import jax
import jax.numpy as jnp
import numpy as np
from jax import lax
from jax.experimental import pallas as pl
from jax.experimental.pallas import tpu as pltpu

F32 = jnp.float32
BF16 = jnp.bfloat16
HI = lax.Precision.HIGHEST
NORM_EPS = 1e-6
V7X_VMEM_LIMIT_BYTES = 56 * 1024 * 1024
N_MOD_ROWS = 16
NORM_ROWS = 256
SCAN_PAIR = 2

D_MODEL = 2048
HALF_W = D_MODEL // 2
GRID_W = 64
HY_W = HALF_W
HY_ORDER = 2
HY_BANDS = 16
HY_FAST_DECAY = 0.3
HY_SLOW_DECAY = 1.5
HY_DECAY_TARGET = 1e-2
GLA_HEADS = 4
GLA_DK = HALF_W // (2 * GLA_HEADS)
GLA_DV = HALF_W // GLA_HEADS
GLA_RANK = 16
GLA_TAU = 16.0
GLA_CHUNK = 64
ML_HEADS = 4
ML_DH = HALF_W // ML_HEADS
ML_CHUNK = 128
DA_HEADS = 8
DA_DH = HALF_W // (2 * DA_HEADS)
DA_DV = 2 * DA_DH
Q_BLOCK = 128
ROPE_BASE = 10000.0
N_EXPERTS = 16
EC_CAPACITY = 2
ML_IN = 4 * HALF_W + 4 * ML_HEADS


def _cparams(*sem, vmem_mib=None):
    limit = V7X_VMEM_LIMIT_BYTES if vmem_mib is None else vmem_mib * 1024 * 1024
    return pltpu.CompilerParams(dimension_semantics=sem, vmem_limit_bytes=limit)


def _dot(a, b):
    return jnp.dot(a, b, preferred_element_type=F32)


def _ada_kernel(c_ref, w_ref, b_ref, o_ref):
    c = c_ref[...]
    a = c * jax.nn.sigmoid(c)
    a_hi = a.astype(BF16)
    a_lo = (a - a_hi.astype(F32)).astype(BF16)
    w = w_ref[...]
    w_hi = w.astype(BF16)
    w_lo = (w - w_hi.astype(F32)).astype(BF16)
    n = a.shape[0]
    r1 = _dot(jnp.concatenate([a_hi, a_lo], axis=0), w_hi)
    r2 = _dot(a_hi, w_lo)
    o_ref[...] = r1[:n] + r1[n:] + r2 + b_ref[...]


def ada_mod(cvec, ada_w, ada_b, tn=512):
    n_lyr, d, n = ada_w.shape
    r = cvec.shape[0]
    return pl.pallas_call(
        _ada_kernel,
        grid=(n_lyr, n // tn),
        in_specs=[pl.BlockSpec((r, d), lambda l, j: (0, 0)),
                  pl.BlockSpec((None, d, tn), lambda l, j: (l, 0, j)),
                  pl.BlockSpec((None, 1, tn), lambda l, j: (l, 0, j))],
        out_specs=pl.BlockSpec((None, r, tn), lambda l, j: (l, 0, j)),
        out_shape=jax.ShapeDtypeStruct((n_lyr, r, n), F32),
        compiler_params=_cparams("parallel", "parallel"),
        name="ada_mod",
    )(cvec, ada_w, ada_b.reshape(n_lyr, 1, n))


def _mod_spec(d, chunk, row_fn, ngrid):
    if ngrid == 2:
        return pl.BlockSpec((None, 1, d), lambda i, j: (row_fn(i), 0, chunk))
    return pl.BlockSpec((None, 1, d), lambda i: (row_fn(i), 0, chunk))


def _norm_mod(x, g, shift, scale):
    y = x * lax.rsqrt(jnp.mean(x * x, axis=-1, keepdims=True) + NORM_EPS) * g
    return y * (1.0 + scale) + shift


def _inproj_kernel(h_ref, g_ref, sh_ref, sc_ref, w_ref, o_ref, a_scr):
    @pl.when(pl.program_id(1) == 0)
    def _():
        rows = min(NORM_ROWS, h_ref.shape[0])

        def body(r, carry):
            sl = pl.ds(pl.multiple_of(r * rows, rows), rows)
            a_scr[sl, :] = _norm_mod(h_ref[sl, :], g_ref[...], sh_ref[...], sc_ref[...]).astype(BF16)
            return carry

        lax.fori_loop(0, h_ref.shape[0] // rows, body, 0)

    o_ref[...] = _dot(a_scr[...], w_ref[...])


def in_proj(h, g, mod, row_fn, w, tm=2048, tn=512):
    t, d = h.shape
    n = w.shape[1]
    tm = min(tm, t)
    if n % tn:
        tn = 256 if n % 256 == 0 else 128
    return pl.pallas_call(
        _inproj_kernel,
        grid=(t // tm, n // tn),
        in_specs=[pl.BlockSpec((tm, d), lambda i, j: (i, 0), pipeline_mode=pl.Buffered(1)),
                  pl.BlockSpec((1, d), lambda i, j: (0, 0)),
                  _mod_spec(d, 0, row_fn, 2), _mod_spec(d, 1, row_fn, 2),
                  pl.BlockSpec((d, tn), lambda i, j: (0, j))],
        out_specs=pl.BlockSpec((tm, tn), lambda i, j: (i, j)),
        out_shape=jax.ShapeDtypeStruct((t, n), F32),
        scratch_shapes=[pltpu.VMEM((tm, d), BF16)],
        compiler_params=_cparams("parallel", "arbitrary", vmem_mib=48),
        name="in_proj",
    )(h, g, mod, mod, w)


def _outproj_kernel(ya_ref, yb_ref, w_ref, h_ref, gate_ref, o_ref):
    ka = ya_ref.shape[1]
    acc = _dot(ya_ref[...], w_ref[:ka, :]) + _dot(yb_ref[...], w_ref[ka:, :])
    o_ref[...] = h_ref[...] + gate_ref[...] * acc


def out_proj(ya, yb, w, h, mod, row_fn, tm=1024, tn=512):
    t, d = h.shape
    ka, kb = ya.shape[1], yb.shape[1]
    tm, tn = min(tm, t), min(tn, d)
    return pl.pallas_call(
        _outproj_kernel,
        grid=(t // tm, d // tn),
        in_specs=[pl.BlockSpec((tm, ka), lambda i, j: (i, 0)),
                  pl.BlockSpec((tm, kb), lambda i, j: (i, 0)),
                  pl.BlockSpec((ka + kb, tn), lambda i, j: (0, j)),
                  pl.BlockSpec((tm, tn), lambda i, j: (i, j)),
                  pl.BlockSpec((None, 1, tn), lambda i, j: (row_fn(i), 0, 2 * (d // tn) + j))],
        out_specs=pl.BlockSpec((tm, tn), lambda i, j: (i, j)),
        out_shape=jax.ShapeDtypeStruct((t, d), F32),
        compiler_params=_cparams("parallel", "parallel", vmem_mib=32),
        name="out_proj",
    )(ya, yb, w, h, mod)


def _moeprep_kernel(h_ref, g_ref, sh_ref, sc_ref, rw_ref, a_ref, lg_ref):
    a = _norm_mod(h_ref[...], g_ref[...], sh_ref[...], sc_ref[...])
    a_ref[...] = a.astype(BF16)
    lg_ref[...] = jnp.dot(a, rw_ref[...], precision=HI, preferred_element_type=F32)


def moe_prep(h, g, mod, row_fn, router_w, tm=512):
    t, d = h.shape
    e = router_w.shape[1]
    tm = min(tm, t)
    return pl.pallas_call(
        _moeprep_kernel,
        grid=(t // tm,),
        in_specs=[pl.BlockSpec((tm, d), lambda i: (i, 0)),
                  pl.BlockSpec((1, d), lambda i: (0, 0)),
                  _mod_spec(d, 3, row_fn, 1), _mod_spec(d, 4, row_fn, 1),
                  pl.BlockSpec((d, e), lambda i: (0, 0))],
        out_specs=[pl.BlockSpec((tm, d), lambda i: (i, 0)), pl.BlockSpec((tm, e), lambda i: (i, 0))],
        out_shape=[jax.ShapeDtypeStruct((t, d), BF16), jax.ShapeDtypeStruct((t, e), F32)],
        compiler_params=_cparams("parallel"),
        name="moe_prep",
    )(h, g, mod, mod, router_w)


def _route_kernel(lg_ref, pos_ref, gate_ref, *, cap):
    lg = lg_ref[...]
    ts, ne = lg.shape
    ex = jnp.exp(lg - jnp.max(lg, axis=-1, keepdims=True))
    aff = ex / jnp.sum(ex, axis=-1, keepdims=True)
    bits = pltpu.bitcast(aff, jnp.int32)

    def bisect(i, thr):
        cand = thr | jnp.left_shift(jnp.int32(1), 30 - i)
        cnt = jnp.sum((bits >= cand).astype(F32), axis=0, keepdims=True)
        return jnp.where(cnt >= cap, cand, thr)

    thr = lax.fori_loop(0, 31, bisect, jnp.zeros((1, ne), jnp.int32))
    gt = bits > thr
    eq = bits == thr
    n_gt = jnp.sum(gt.astype(F32), axis=0, keepdims=True)
    r = lax.broadcasted_iota(jnp.int32, (ts, ts), 0)
    c = lax.broadcasted_iota(jnp.int32, (ts, ts), 1)
    tri = (c < r).astype(BF16)
    eq_rank = _dot(tri, eq.astype(BF16))
    sel = gt | (eq & (eq_rank < cap - n_gt))
    pos = _dot(tri, sel.astype(BF16))
    pos_ref[...] = jnp.where(sel, pos, -1.0)
    gate_ref[...] = jnp.where(sel, aff, 0.0)


def moe_route(logits, n_samples, cap):
    t, e = logits.shape
    ts = t // n_samples
    spec = pl.BlockSpec((ts, e), lambda s: (s, 0))
    return pl.pallas_call(
        functools.partial(_route_kernel, cap=cap),
        grid=(n_samples,),
        in_specs=[spec],
        out_specs=[spec, spec],
        out_shape=[jax.ShapeDtypeStruct((t, e), F32)] * 2,
        compiler_params=_cparams("parallel"),
        name="moe_route",
    )(logits)


def _slot_onehot(pos_col, cap):
    slots = lax.broadcasted_iota(jnp.int32, (1, cap), 1).astype(F32)
    return (pos_col == slots).astype(BF16)


def _gather_kernel(pos_ref, a_ref, x_ref, *, cap):
    pos = pos_ref[...]
    a = a_ref[...]
    for e in range(pos.shape[1]):
        pt = _slot_onehot(pos[:, e:e + 1], cap)
        x = lax.dot_general(pt, a, (((0,), (0,)), ((), ())), preferred_element_type=F32)
        x_ref[e] = x.astype(BF16)


def moe_gather(pos, a, n_samples, cap, tn=1024):
    t, d = a.shape
    e = pos.shape[1]
    ts = t // n_samples
    tn = min(tn, d)
    return pl.pallas_call(
        functools.partial(_gather_kernel, cap=cap),
        grid=(n_samples, d // tn),
        in_specs=[pl.BlockSpec((ts, e), lambda s, j: (s, 0)), pl.BlockSpec((ts, tn), lambda s, j: (s, j))],
        out_specs=pl.BlockSpec((e, cap, tn), lambda s, j: (0, s, j)),
        out_shape=jax.ShapeDtypeStruct((e, n_samples * cap, d), BF16),
        compiler_params=_cparams("parallel", "parallel"),
        name="moe_gather",
    )(pos, a)


def _ffn_kernel(*refs, nx, nff):
    x_refs = refs[:nx]
    w1_ref, w3_ref, w2_ref = refs[nx:nx + 3]
    y_refs = refs[nx + 3:2 * nx + 3]
    hid_refs = refs[2 * nx + 3:]
    j = pl.program_id(1)

    @pl.when(j < nff)
    def _():
        w1 = w1_ref[...].astype(BF16)
        w3 = w3_ref[...].astype(BF16)
        for x_ref, hid_ref in zip(x_refs, hid_refs):
            x = x_ref[...]
            h1 = _dot(x, w1)
            h3 = _dot(x, w3)
            hid_ref[j] = (h1 * jax.nn.sigmoid(h1) * h3).astype(BF16)

    @pl.when(j >= nff)
    def _():
        w2 = w2_ref[...].astype(BF16)
        tf = w2.shape[0] // nff
        for y_ref, hid_ref in zip(y_refs, hid_refs):
            acc = _dot(hid_ref[0], w2[:tf])
            for jj in range(1, nff):
                acc += _dot(hid_ref[jj], w2[jj * tf:(jj + 1) * tf])
            y_ref[...] = acc.astype(y_ref.dtype)


def moe_ffn(xs, w1, w3, w2, layer, tf=256, tn=256):
    _, ne, d, ff = w1.shape
    tf, tn = min(tf, ff), min(tn, d)
    nff, nd = ff // tf, d // tn
    nx = len(xs)
    x_specs = [pl.BlockSpec((None, x.shape[1], d), lambda e, j: (e, 0, 0)) for x in xs]
    up_spec = pl.BlockSpec((None, None, d, tf), lambda e, j: (layer, e, 0, jnp.minimum(j, nff - 1)))
    down_spec = pl.BlockSpec((None, None, ff, tn), lambda e, j: (layer, e, 0, jnp.maximum(j - nff, 0)))
    y_specs = [pl.BlockSpec((None, x.shape[1], tn), lambda e, j: (e, 0, jnp.maximum(j - nff, 0))) for x in xs]
    return pl.pallas_call(
        functools.partial(_ffn_kernel, nx=nx, nff=nff),
        grid=(ne, nff + nd),
        in_specs=x_specs + [up_spec, up_spec, down_spec],
        out_specs=y_specs,
        out_shape=[jax.ShapeDtypeStruct(x.shape, BF16) for x in xs],
        scratch_shapes=[pltpu.VMEM((nff, x.shape[1], tf), BF16) for x in xs],
        compiler_params=_cparams("parallel", "arbitrary"),
        name="moe_ffn",
    )(*xs, w1, w3, w2)


def _combine_kernel(pos_ref, gate_ref, y_ref, h_ref, m_ref, o_ref, pt_scr, *, cap):
    ne = pos_ref.shape[1]

    @pl.when(pl.program_id(1) == 0)
    def _():
        pos = pos_ref[...]
        for e in range(ne):
            pt_scr[e] = _slot_onehot(pos[:, e:e + 1], cap)

    gate = gate_ref[...]
    acc = gate[:, 0:1] * _dot(pt_scr[0], y_ref[0])
    for e in range(1, ne):
        acc += gate[:, e:e + 1] * _dot(pt_scr[e], y_ref[e])
    o_ref[...] = h_ref[...] + m_ref[...] * acc


def moe_combine(pos, gate, y, h, mod, row_fn, n_samples, cap, tn=512):
    t, d = h.shape
    e = pos.shape[1]
    ts = t // n_samples
    tn = min(tn, d)
    return pl.pallas_call(
        functools.partial(_combine_kernel, cap=cap),
        grid=(n_samples, d // tn),
        in_specs=[pl.BlockSpec((ts, e), lambda s, j: (s, 0)),
                  pl.BlockSpec((ts, e), lambda s, j: (s, 0)),
                  pl.BlockSpec((e, cap, tn), lambda s, j: (0, s, j)),
                  pl.BlockSpec((ts, tn), lambda s, j: (s, j)),
                  pl.BlockSpec((None, 1, tn), lambda s, j: (row_fn(s), 0, 5 * (d // tn) + j))],
        out_specs=pl.BlockSpec((ts, tn), lambda s, j: (s, j)),
        out_shape=jax.ShapeDtypeStruct((t, d), F32),
        scratch_shapes=[pltpu.VMEM((e, ts, cap), BF16)],
        compiler_params=_cparams("parallel", "arbitrary"),
        name="moe_combine",
    )(pos, gate, y, h, mod)


def _rms_kernel(x_ref, g_ref, o_ref):
    x = x_ref[...]
    o_ref[...] = x * lax.rsqrt(jnp.mean(x * x, axis=-1, keepdims=True) + NORM_EPS) * g_ref[...]


def final_norm(h, g, tm=512):
    t, d = h.shape
    return pl.pallas_call(
        _rms_kernel,
        grid=(t // tm,),
        in_specs=[pl.BlockSpec((tm, d), lambda i: (i, 0)), pl.BlockSpec((1, d), lambda i: (0, 0))],
        out_specs=pl.BlockSpec((tm, d), lambda i: (i, 0)),
        out_shape=jax.ShapeDtypeStruct((t, d), F32),
        compiler_params=_cparams("parallel"),
        name="final_norm",
    )(h, g)


def tile_rows(row_of_sample, ts, tm):
    per = ts // tm
    return lambda i: row_of_sample(i // per)


def moe_layer(streams, g2, mod, router_w, w1, w3, w2, layer):
    routed = []
    for h, row_of_sample, n_samples in streams:
        ts = h.shape[0] // n_samples
        cap = EC_CAPACITY * ts // N_EXPERTS
        tm = min(512, ts)
        a, logits = moe_prep(h, g2, mod, tile_rows(row_of_sample, ts, tm), router_w, tm=tm)
        pos, gate = moe_route(logits, n_samples, cap)
        routed.append((pos, gate, cap, moe_gather(pos, a, n_samples, cap)))
    ys = moe_ffn([r[3] for r in routed], w1, w3, w2, layer)
    return [moe_combine(pos, gate, y, h, mod, row_of_sample, n_samples, cap)
            for (h, row_of_sample, n_samples), (pos, gate, cap, _), y in zip(streams, routed, ys)]


def _dir_tri(n, d):
    i = lax.broadcasted_iota(jnp.int32, (n, n), 0)
    j = lax.broadcasted_iota(jnp.int32, (n, n), 1)
    return (j - i) * (1 - 2 * d) <= 0


def _gla_chunk(q_ref, k_ref, v_ref, c_ref, w2_ref, gb_ref, o_ref, st_ref, d):
    cs = q_ref.shape[0]
    causal = _dir_tri(cs, d)
    logits = jnp.dot(c_ref[...], w2_ref[...], precision=HI, preferred_element_type=F32) + gb_ref[...]
    logg = jax.nn.log_sigmoid(logits) * (1.0 / GLA_TAU)
    b = jnp.dot(causal.astype(F32), logg, precision=HI, preferred_element_type=F32)
    b_last = jnp.sum(logg, axis=0, keepdims=True)
    q = q_ref[...] * (GLA_DK ** -0.5)
    k = k_ref[...]
    q_dec = (q * jnp.exp(b)).astype(BF16)
    k_dec = (k * jnp.exp(-b)).astype(BF16)
    k_end = (k * jnp.exp(b_last - b)).astype(BF16)
    v = v_ref[...].astype(BF16)
    for h in range(GLA_HEADS):
        ks = slice(h * GLA_DK, (h + 1) * GLA_DK)
        vs = slice(h * GLA_DV, (h + 1) * GLA_DV)
        st = st_ref[h]
        att = lax.dot_general(q_dec[:, ks], k_dec[:, ks], (((1,), (1,)), ((), ())), preferred_element_type=F32)
        att = jnp.where(causal, att, 0.0).astype(BF16)
        o = _dot(att, v[:, vs]) + lax.dot_general(q_dec[:, ks], st.astype(BF16), (((1,), (1,)), ((), ())),
                                                  preferred_element_type=F32)
        o_ref[:, vs] = o
        upd = lax.dot_general(v[:, vs], k_end[:, ks], (((0,), (0,)), ((), ())), preferred_element_type=F32)
        st_ref[h] = st * jnp.exp(b_last[:, ks]) + upd


def _gla_kernel(qc_ref, kc_ref, vc_ref, cc_ref, ql_ref, kl_ref, vl_ref, cl_ref, w2_ref, gb_ref,
                oc_ref, ol_ref, st_ref, *, ncc):
    d = pl.program_id(1)
    c = pl.program_id(2)

    @pl.when(c == 0)
    def _():
        st_ref[...] = jnp.zeros_like(st_ref)

    @pl.when(c < ncc)
    def _():
        for s in range(qc_ref.shape[0]):
            _gla_chunk(qc_ref.at[s], kc_ref.at[s], vc_ref.at[s], cc_ref.at[s], w2_ref, gb_ref, oc_ref.at[s],
                       st_ref.at[s], d)

    @pl.when(c >= ncc)
    def _():
        for s in range(ql_ref.shape[0]):
            _gla_chunk(ql_ref.at[s], kl_ref.at[s], vl_ref.at[s], cl_ref.at[s], w2_ref, gb_ref, ol_ref.at[s],
                       st_ref.at[s], d)


def _scan_chunk_maps(n_samples_chunks_ctx, n_samples_chunks_lat):
    ncc, nlc = n_samples_chunks_ctx, n_samples_chunks_lat

    def ctx_blk(b, d, c):
        i = jnp.minimum(c, ncc - 1)
        return b * ncc + jnp.where(d == 0, i, ncc - 1 - i)

    def lat_blk(b, d, c):
        i = jnp.maximum(c - ncc, 0)
        return b * nlc + jnp.where(d == 0, i, nlc - 1 - i)

    return ctx_blk, lat_blk


def gla_scan_call(p_ctx, p_lat, w2_full, gate_b, n_samples, col0, col_codes):
    cs = GLA_CHUNK
    hk, hv = GLA_HEADS * GLA_DK, GLA_HEADS * GLA_DV
    ncc, nlc = p_ctx.shape[0] // n_samples // cs, p_lat.shape[0] // n_samples // cs
    ctx_blk, lat_blk = _scan_chunk_maps(ncc, nlc)
    qb, kb, vb = col0 // hk, col0 // hk + 1, (col0 + 2 * hk) // hv
    cb = col_codes // LANES
    pair = SCAN_PAIR if n_samples % SCAN_PAIR == 0 else 1
    p_ctx = p_ctx.reshape(pair, p_ctx.shape[0] // pair, p_ctx.shape[1])
    p_lat = p_lat.reshape(pair, p_lat.shape[0] // pair, p_lat.shape[1])

    def specs(blk):
        return [pl.BlockSpec((pair, cs, hk), lambda b, d, c: (0, blk(b, d, c), qb)),
                pl.BlockSpec((pair, cs, hk), lambda b, d, c: (0, blk(b, d, c), kb)),
                pl.BlockSpec((pair, cs, hv), lambda b, d, c: (0, blk(b, d, c), vb)),
                pl.BlockSpec((pair, cs, 128), lambda b, d, c: (0, blk(b, d, c), cb))]

    o_ctx, o_lat = pl.pallas_call(
        functools.partial(_gla_kernel, ncc=ncc),
        grid=(n_samples // pair, 2, ncc + nlc),
        in_specs=specs(ctx_blk) + specs(lat_blk) + [
            pl.BlockSpec((None, 128, hk), lambda b, d, c: (d, 0, 0)),
            pl.BlockSpec((None, 1, hk), lambda b, d, c: (d, 0, 0))],
        out_specs=[pl.BlockSpec((None, pair, cs, hv), lambda b, d, c: (d, 0, ctx_blk(b, d, c), 0)),
                   pl.BlockSpec((None, pair, cs, hv), lambda b, d, c: (d, 0, lat_blk(b, d, c), 0))],
        out_shape=[jax.ShapeDtypeStruct((2,) + p_ctx.shape[:2] + (hv,), F32),
                   jax.ShapeDtypeStruct((2,) + p_lat.shape[:2] + (hv,), F32)],
        scratch_shapes=[pltpu.VMEM((pair, GLA_HEADS, GLA_DV, GLA_DK), F32)],
        compiler_params=_cparams("parallel", "arbitrary", "arbitrary", vmem_mib=32),
        name="gla_scan",
    )(p_ctx, p_ctx, p_ctx, p_ctx, p_lat, p_lat, p_lat, p_lat, w2_full, gate_b)
    return o_ctx.reshape(2, -1, hv), o_lat.reshape(2, -1, hv)


def _headnorm_gate_kernel(o_ref, r_ref, g_ref, y_ref, *, n_heads, act):
    o = o_ref[0] + o_ref[1]
    r = r_ref[...]
    gate = r * jax.nn.sigmoid(r) if act == "silu" else jax.nn.sigmoid(r)
    dh = o.shape[1] // n_heads
    for h in range(n_heads):
        s = slice(h * dh, (h + 1) * dh)
        oh = o[:, s]
        yh = oh * lax.rsqrt(jnp.mean(oh * oh, axis=-1, keepdims=True) + NORM_EPS) * g_ref[...]
        y_ref[:, s] = (yh * gate[:, s]).astype(y_ref.dtype)


def headnorm_gate(o2, p, r_col, norm_g, n_heads, act, tm=512):
    _, t, w = o2.shape
    tm = min(tm, t)
    return pl.pallas_call(
        functools.partial(_headnorm_gate_kernel, n_heads=n_heads, act=act),
        grid=(t // tm,),
        in_specs=[pl.BlockSpec((2, tm, w), lambda i: (0, i, 0)),
                  pl.BlockSpec((tm, w), lambda i: (i, r_col // w)),
                  pl.BlockSpec((1, w // n_heads), lambda i: (0, 0))],
        out_specs=pl.BlockSpec((tm, w), lambda i: (i, 0)),
        out_shape=jax.ShapeDtypeStruct((t, w), BF16),
        compiler_params=_cparams("parallel", vmem_mib=32),
        name="headnorm_gate",
    )(o2, p, norm_g.reshape(1, -1))


def gla_mixer(p_ctx, p_lat, gate_w2, gate_b, norm_g, n_samples, col0):
    hk, hv = GLA_HEADS * GLA_DK, GLA_HEADS * GLA_DV
    w2_full = jnp.zeros((2, 128, hk), F32)
    for d in range(2):
        w2_full = w2_full.at[d, d * GLA_RANK:(d + 1) * GLA_RANK].set(gate_w2[d])
    r_col = col0 + 2 * hk + hv
    o_ctx, o_lat = gla_scan_call(p_ctx, p_lat, w2_full, gate_b.reshape(2, 1, hk), n_samples, col0, r_col + hv)
    return (headnorm_gate(o_ctx, p_ctx, r_col, norm_g, GLA_HEADS, "silu"),
            headnorm_gate(o_lat, p_lat, r_col, norm_g, GLA_HEADS, "silu"))


ROPE_NF = DA_DH // 4
LANES = 128


def _rope_table_kernel(cos_ref, sin_ref):
    n = cos_ref.shape[0]
    t = lax.broadcasted_iota(jnp.int32, (n, LANES), 0)
    j = lax.broadcasted_iota(jnp.int32, (n, LANES), 1)
    dd = j % DA_DH
    is_col = (dd // (2 * ROPE_NF)) == 1
    is_xb = ((dd // ROPE_NF) % 2) == 1
    f = (dd % ROPE_NF).astype(F32)
    inv = jnp.exp(f * (-math.log(ROPE_BASE) / ROPE_NF))
    pos = jnp.where(is_col, t % GRID_W, t // GRID_W).astype(F32)
    ang = pos * inv
    cos_ref[...] = jnp.cos(ang)
    sin_ref[...] = jnp.where(is_xb, 1.0, -1.0) * jnp.sin(ang)


def rope_tables(n):
    out = jax.ShapeDtypeStruct((n, LANES), F32)
    return pl.pallas_call(_rope_table_kernel, out_shape=[out, out], name="rope_tables")()


def _rope(x, cos, sin_signed):
    lane = lax.broadcasted_iota(jnp.int32, x.shape, 1)
    is_xb = ((lane // ROPE_NF) % 2) == 1
    partner = jnp.where(is_xb, pltpu.roll(x, ROPE_NF, 1), pltpu.roll(x, LANES - ROPE_NF, 1))
    return x * cos + partner * sin_signed


def _nt(a, b):
    return lax.dot_general(a, b, (((1,), (1,)), ((), ())), preferred_element_type=F32)


def _diffattn_kernel(q_ref, kl_ref, vl_ref, kc_ref, vc_ref, cq_ref, sq_ref, ck_ref, sk_ref, lam_ref, g_ref, o_ref,
                     kl_scr, vl_scr, kc_scr, vc_scr, *, lam_init):
    @pl.when(pl.program_id(2) == 0)
    def _():
        kl_scr[...] = _rope(kl_ref[...], ck_ref[...], sk_ref[...]).astype(BF16)
        kc_scr[...] = kc_ref[...].astype(BF16)
        vl_scr[:, :LANES] = vl_ref[...].astype(BF16)
        vl_scr[:, LANES:] = jnp.ones(vl_ref.shape, BF16)
        vc_scr[:, :LANES] = vc_ref[...].astype(BF16)
        vc_scr[:, LANES:] = jnp.ones(vc_ref.shape, BF16)

    lv = lam_ref[...]
    lam = (jnp.exp(jnp.sum(lv[0:1] * lv[1:2], keepdims=True)) - jnp.exp(jnp.sum(lv[2:3] * lv[3:4], keepdims=True))
           + lam_init)
    q = _rope(q_ref[...], cq_ref[...], sq_ref[...]) * (DA_DH ** -0.5)
    first = lax.broadcasted_iota(jnp.int32, q.shape, 1) < DA_DH
    attn = []
    for comp in range(2):
        qc = jnp.where(first == (comp == 0), q, 0.0).astype(BF16)
        s_l = _nt(qc, kl_scr[...])
        s_c = _nt(qc, kc_scr[...])
        m = jnp.maximum(jnp.max(s_l, axis=-1, keepdims=True), jnp.max(s_c, axis=-1, keepdims=True))
        acc = _dot(jnp.exp(s_l - m).astype(BF16), vl_scr[...]) + _dot(jnp.exp(s_c - m).astype(BF16), vc_scr[...])
        attn.append(acc[:, :LANES] / acc[:, LANES:])
    o = attn[0] - lam * attn[1]
    o = o * lax.rsqrt(jnp.mean(o * o, axis=-1, keepdims=True) + NORM_EPS) * g_ref[...]
    o_ref[...] = (o * (1.0 - lam_init)).astype(o_ref.dtype)


def diff_attention_call(p_ctx, p_lat, n_samples, col_q, lam_vecs, norm_g, lam_init, tq=512):
    lt, lc = p_lat.shape[0] // n_samples, p_ctx.shape[0] // n_samples
    tq = min(tq, lt)
    nq = lt // tq
    w = DA_HEADS * DA_DV
    qb, kb, vb = col_q // LANES, (col_q + w) // LANES, (col_q + 2 * w) // LANES
    cos, sin = rope_tables(lt)
    return pl.pallas_call(
        functools.partial(_diffattn_kernel, lam_init=lam_init),
        grid=(n_samples, DA_HEADS, nq),
        in_specs=[pl.BlockSpec((tq, LANES), lambda b, h, i: (b * nq + i, qb + h)),
                  pl.BlockSpec((lt, LANES), lambda b, h, i: (b, kb + h)),
                  pl.BlockSpec((lt, LANES), lambda b, h, i: (b, vb + h)),
                  pl.BlockSpec((lc, LANES), lambda b, h, i: (b, kb + h)),
                  pl.BlockSpec((lc, LANES), lambda b, h, i: (b, vb + h)),
                  pl.BlockSpec((tq, LANES), lambda b, h, i: (i, 0)),
                  pl.BlockSpec((tq, LANES), lambda b, h, i: (i, 0)),
                  pl.BlockSpec((lt, LANES), lambda b, h, i: (0, 0)),
                  pl.BlockSpec((lt, LANES), lambda b, h, i: (0, 0)),
                  pl.BlockSpec(lam_vecs.shape, lambda b, h, i: (0, 0)),
                  pl.BlockSpec((1, DA_DV), lambda b, h, i: (0, 0))],
        out_specs=pl.BlockSpec((tq, LANES), lambda b, h, i: (b * nq + i, h)),
        out_shape=jax.ShapeDtypeStruct((p_lat.shape[0], w), BF16),
        scratch_shapes=[pltpu.VMEM((lt, LANES), BF16), pltpu.VMEM((lt, 2 * LANES), BF16),
                        pltpu.VMEM((lc, LANES), BF16), pltpu.VMEM((lc, 2 * LANES), BF16)],
        compiler_params=_cparams("parallel", "parallel", "arbitrary", vmem_mib=48),
        name="diff_attention",
    )(p_lat, p_lat, p_lat, p_ctx, p_ctx, cos, sin, cos, sin, lam_vecs, norm_g.reshape(1, DA_DV))


def _convsilu_kernel(u_ref, w_ref, b_ref, s_ref, o_ref, *, silu):
    u = u_ref[...]
    n = u.shape[0]
    row = lax.broadcasted_iota(jnp.int32, u.shape, 0)
    prev = jnp.where(row == 0, 0.0, pltpu.roll(u, 1, 0))
    nxt = jnp.where(row == n - 1, 0.0, pltpu.roll(u, n - 1, 0))
    w = w_ref[...]
    y = w[0:1] * prev + w[1:2] * u + w[2:3] * nxt + b_ref[...]
    if silu:
        y = y * jax.nn.sigmoid(y)
    o_ref[...] = y * s_ref[...]


def conv_silu(p, n_samples, width, conv_w, conv_b, col_scale, tn=512, silu=True):
    ls = p.shape[0] // n_samples
    return pl.pallas_call(
        functools.partial(_convsilu_kernel, silu=silu),
        grid=(n_samples, width // tn),
        in_specs=[pl.BlockSpec((ls, tn), lambda s, j: (s, j)),
                  pl.BlockSpec((conv_w.shape[0], tn), lambda s, j: (0, j)),
                  pl.BlockSpec((1, tn), lambda s, j: (0, j)),
                  pl.BlockSpec((1, tn), lambda s, j: (0, j))],
        out_specs=pl.BlockSpec((ls, tn), lambda s, j: (s, j)),
        out_shape=jax.ShapeDtypeStruct((p.shape[0], width), F32),
        compiler_params=_cparams("parallel", "parallel", vmem_mib=32),
        name="conv_silu",
    )(p, conv_w, conv_b.reshape(1, -1), col_scale.reshape(1, -1))


def _mlstm_chunk(q_ref, k_ref, v_ref, g_ref, gb_ref, o_ref, c_ref, n_ref, m_ref, d, emit):
    cs = q_ref.shape[0]
    causal = _dir_tri(cs, d)
    g = g_ref[...] + gb_ref[...]
    lane = lax.broadcasted_iota(jnp.int32, g.shape, 1)
    gl = jnp.where((lane // ML_HEADS) % 2 == 1, jax.nn.log_sigmoid(g), g)
    bcum = jnp.dot(causal.astype(F32), gl, precision=HI, preferred_element_type=F32)
    gl_t = gl.T
    bcum_t = bcum.T
    q = q_ref[...].astype(BF16)
    k = k_ref[...]
    kb = k.astype(BF16)
    v = v_ref[...]

    def pick_col(a, idx):
        return jnp.where(d == 0, a[:, idx:idx + 1], a[:, 2 * ML_HEADS + idx:2 * ML_HEADS + idx + 1])

    def pick_row(a, idx):
        return jnp.where(d == 0, a[idx:idx + 1, :], a[2 * ML_HEADS + idx:2 * ML_HEADS + idx + 1, :])

    for h in range(ML_HEADS):
        hs = slice(h * ML_DH, (h + 1) * ML_DH)
        ic_c, ic_r = pick_col(gl, h), pick_row(gl_t, h)
        fc_c = pick_col(gl, ML_HEADS + h)
        b_c, b_r = pick_col(bcum, ML_HEADS + h), pick_row(bcum_t, ML_HEADS + h)
        b_last = jnp.sum(fc_c, axis=0, keepdims=True)
        m = m_ref[h][:, 0:1]
        cm = c_ref[h]
        nv = n_ref[h]
        w_end_c = b_last - b_c + ic_c
        w_end_r = b_last - b_r + ic_r
        m_new = jnp.maximum(b_last + m, jnp.max(w_end_r, axis=-1, keepdims=True))
        keep = jnp.exp(b_last + m - m_new)
        w_c = jnp.exp(w_end_c - m_new)
        vw = (v[:, hs] * w_c).astype(BF16)
        c_ref[h] = keep * cm + lax.dot_general(vw, kb[:, hs], (((0,), (0,)), ((), ())), preferred_element_type=F32)
        n_ref[h] = keep * nv + jnp.sum(w_c * k[:, hs], axis=0, keepdims=True)
        m_ref[h] = jnp.broadcast_to(m_new, m_ref.shape[1:])
        if emit:
            a_c = b_c + m
            dlog = jnp.where(causal, b_c - b_r + ic_r, -jnp.inf)
            m_t = jnp.maximum(a_c, jnp.max(dlog, axis=-1, keepdims=True))
            sc = _nt(q[:, hs], kb[:, hs]) * jnp.exp(dlog - m_t)
            aw = jnp.exp(a_c - m_t)
            num = _dot(sc.astype(BF16), v[:, hs].astype(BF16)) + aw * _nt(q[:, hs], cm.astype(BF16))
            qf = q_ref[:, hs]
            den = jnp.sum(sc, axis=-1, keepdims=True) + aw * jnp.sum(qf * nv, axis=-1, keepdims=True)
            o_ref[:, hs] = num / jnp.maximum(jnp.abs(den), jnp.exp(-m_t))


def _mlstm_kernel(qc_ref, kc_ref, vc_ref, gc_ref, ql_ref, kl_ref, vl_ref, gl_ref, gb_ref,
                  ol_ref, c_ref, n_ref, m_ref, *, ncc):
    d = pl.program_id(1)
    c = pl.program_id(2)

    @pl.when(c == 0)
    def _():
        c_ref[...] = jnp.zeros_like(c_ref)
        n_ref[...] = jnp.zeros_like(n_ref)
        m_ref[...] = jnp.zeros_like(m_ref)

    @pl.when(c < ncc)
    def _():
        for s in range(qc_ref.shape[0]):
            _mlstm_chunk(qc_ref.at[s], kc_ref.at[s], vc_ref.at[s], gc_ref.at[s], gb_ref, None,
                         c_ref.at[s], n_ref.at[s], m_ref.at[s], d, False)

    @pl.when(c >= ncc)
    def _():
        for s in range(ql_ref.shape[0]):
            _mlstm_chunk(ql_ref.at[s], kl_ref.at[s], vl_ref.at[s], gl_ref.at[s], gb_ref, ol_ref.at[s],
                         c_ref.at[s], n_ref.at[s], m_ref.at[s], d, True)


def mlstm_scan_call(qk_ctx, qk_lat, p_ctx, p_lat, gate_bias, n_samples, col_v, col_g):
    cs = ML_CHUNK
    w = ML_HEADS * ML_DH
    ncc, nlc = p_ctx.shape[0] // n_samples // cs, p_lat.shape[0] // n_samples // cs
    ctx_blk, lat_blk = _scan_chunk_maps(ncc, nlc)
    vb = col_v // w
    gcol = col_g // LANES
    pair = SCAN_PAIR if n_samples % SCAN_PAIR == 0 else 1
    halves = lambda a: a.reshape(pair, a.shape[0] // pair, a.shape[1])
    qk_ctx, qk_lat, p_ctx, p_lat = halves(qk_ctx), halves(qk_lat), halves(p_ctx), halves(p_lat)

    def specs(blk):
        return [pl.BlockSpec((pair, cs, w), lambda b, d, c: (0, blk(b, d, c), 0)),
                pl.BlockSpec((pair, cs, w), lambda b, d, c: (0, blk(b, d, c), 1)),
                pl.BlockSpec((pair, cs, w), lambda b, d, c: (0, blk(b, d, c), vb)),
                pl.BlockSpec((pair, cs, LANES), lambda b, d, c: (0, blk(b, d, c), gcol))]

    out = pl.pallas_call(
        functools.partial(_mlstm_kernel, ncc=ncc),
        grid=(n_samples // pair, 2, ncc + nlc),
        in_specs=specs(ctx_blk) + specs(lat_blk) + [pl.BlockSpec((1, LANES), lambda b, d, c: (0, 0))],
        out_specs=pl.BlockSpec((None, pair, cs, w), lambda b, d, c: (d, 0, lat_blk(b, d, c), 0)),
        out_shape=jax.ShapeDtypeStruct((2,) + p_lat.shape[:2] + (w,), F32),
        scratch_shapes=[pltpu.VMEM((pair, ML_HEADS, ML_DH, ML_DH), F32), pltpu.VMEM((pair, ML_HEADS, 1, ML_DH), F32),
                        pltpu.VMEM((pair, ML_HEADS, 1, LANES), F32)],
        compiler_params=_cparams("parallel", "arbitrary", "arbitrary", vmem_mib=32),
        name="mlstm_scan",
    )(qk_ctx, qk_ctx, p_ctx, p_ctx, qk_lat, qk_lat, p_lat, p_lat, gate_bias)
    return out.reshape(2, -1, w)


def mlstm_mixer(p_ctx, p_lat, conv_w, conv_b, igate_b, fgate_b, norm_g, n_samples, col_g):
    w = ML_HEADS * ML_DH
    scale = jnp.concatenate([jnp.ones((w,), F32), jnp.full((w,), ML_DH ** -0.5, F32)])
    qk_ctx = conv_silu(p_ctx, n_samples, 2 * w, conv_w, conv_b, scale)
    qk_lat = conv_silu(p_lat, n_samples, 2 * w, conv_w, conv_b, scale)
    bias = jnp.concatenate([jnp.stack([igate_b[d], fgate_b[d]]).reshape(-1) for d in range(2)])
    gate_bias = jnp.zeros((1, LANES), F32).at[0, :bias.shape[0]].set(bias)
    o2 = mlstm_scan_call(qk_ctx, qk_lat, p_ctx, p_lat, gate_bias, n_samples, 2 * w, col_g)
    return headnorm_gate(o2, p_lat, 3 * w, norm_g, ML_HEADS, "sigmoid")


HY_FEAT_PAD = 128
DFT_FREQ_PAD = 128


def _hyfilt_kernel(w1_ref, b1_ref, fr_ref, w2_ref, b2_ref, w3_ref, o_ref, *, seq):
    tn = o_ref.shape[1]
    pos = lax.broadcasted_iota(jnp.int32, (seq, HY_FEAT_PAD), 0).astype(F32)
    lane = lax.broadcasted_iota(jnp.int32, (seq, HY_FEAT_PAD), 1)
    t = pos / max(seq - 1, 1)
    band = 1e-4 + ((lane - 1) % HY_BANDS).astype(F32) * ((HY_BANDS - 1 - 1e-4) / (HY_BANDS - 1))
    ang = (2.0 * math.pi / seq) * pos * band
    feats = jnp.where(lane == 0, t, jnp.where(lane <= HY_BANDS, jnp.cos(ang),
                                              jnp.where(lane <= 2 * HY_BANDS, -jnp.sin(ang), 0.0)))
    fr = fr_ref[...]
    h = jnp.sin(fr[0:1] * (jnp.dot(feats, w1_ref[...], precision=HI, preferred_element_type=F32) + b1_ref[...]))
    h = jnp.sin(fr[1:2] * (jnp.dot(h, w2_ref[...], precision=HI, preferred_element_type=F32) + b2_ref[...]))
    h = jnp.dot(h, w3_ref[...], precision=HI, preferred_element_type=F32)
    col = pl.program_id(0) * tn + lax.broadcasted_iota(jnp.int32, (1, tn), 1)
    chan = col % HY_W
    lo, hi = math.log(HY_DECAY_TARGET) / HY_SLOW_DECAY, math.log(HY_DECAY_TARGET) / HY_FAST_DECAY
    delta = jnp.abs(lo + chan.astype(F32) * ((hi - lo) / (HY_W - 1)))
    filt = h * jnp.exp(-t[:, 0:1] * delta)
    is_bwd = (col // HY_W) % 2 == 1
    o_ref[...] = jnp.where(is_bwd & (pos[:, 0:1] == 0.0), 0.0, filt)


def hyena_filters_call(seq, w1, b1, freq, w2, b2, w3, tn=512):
    nf = w1.shape[1]
    n = w3.shape[1]
    w1p = jnp.zeros((HY_FEAT_PAD, nf), F32).at[:w1.shape[0]].set(w1)
    full = lambda shape: pl.BlockSpec(shape, lambda j: (0, 0))
    return pl.pallas_call(
        functools.partial(_hyfilt_kernel, seq=seq),
        grid=(n // tn,),
        in_specs=[full((HY_FEAT_PAD, nf)), full((1, nf)), full((2, nf)), full((nf, nf)), full((1, nf)),
                  pl.BlockSpec((nf, tn), lambda j: (0, j))],
        out_specs=pl.BlockSpec((seq, tn), lambda j: (0, j)),
        out_shape=jax.ShapeDtypeStruct((seq, n), F32),
        compiler_params=_cparams("parallel", vmem_mib=32),
        name="hyena_filters",
    )(w1p, b1.reshape(1, nf), freq, w2, b2.reshape(1, nf), w3)


def _dft_trig(f, s, n, minus_sin):
    k = (f * s + jnp.where(minus_sin, n // 4, 0)) % n
    return jnp.cos(k.astype(F32) * (2.0 * math.pi / n))


def _spectrum_slot(idx, nfreq):
    half = nfreq // 2
    within = idx % nfreq
    return (idx // nfreq) * half + within % half, within >= half


def _dft_fwd_kernel(o_ref, *, seq, nfreq):
    tr = o_ref.shape[0]
    r = pl.program_id(0) * tr + lax.broadcasted_iota(jnp.int32, (tr, seq), 0)
    s = lax.broadcasted_iota(jnp.int32, (tr, seq), 1)
    f, is_im = _spectrum_slot(r, nfreq)
    o_ref[...] = jnp.where(f <= seq, _dft_trig(f, s, 2 * seq, is_im), 0.0).astype(o_ref.dtype)


def _dft_inv_kernel(o_ref, *, seq, nfreq):
    tc = o_ref.shape[1]
    c = pl.program_id(0) * tc + lax.broadcasted_iota(jnp.int32, (seq, tc), 1)
    t = lax.broadcasted_iota(jnp.int32, (seq, tc), 0)
    f, is_im = _spectrum_slot(c, nfreq)
    wf = jnp.where((f == 0) | (f == seq), 1.0, jnp.where(f < seq, 2.0, 0.0)) * (0.5 / seq)
    o_ref[...] = (wf * _dft_trig(f, t, 2 * seq, is_im)).astype(o_ref.dtype)


def dft_tables(seq):
    nfreq = seq + DFT_FREQ_PAD
    fwd = pl.pallas_call(
        functools.partial(_dft_fwd_kernel, seq=seq, nfreq=nfreq), grid=(2 * nfreq // 128,),
        out_specs=pl.BlockSpec((128, seq), lambda i: (i, 0)),
        out_shape=jax.ShapeDtypeStruct((2 * nfreq, seq), BF16), compiler_params=_cparams("parallel", vmem_mib=32),
        name="dft_fwd_table")()
    inv = pl.pallas_call(
        functools.partial(_dft_inv_kernel, seq=seq, nfreq=nfreq), grid=(2 * nfreq // 128,),
        out_specs=pl.BlockSpec((seq, 128), lambda j: (0, j)),
        out_shape=jax.ShapeDtypeStruct((seq, 2 * nfreq), BF16), compiler_params=_cparams("parallel", vmem_mib=32),
        name="dft_inv_table")()
    return fwd, inv


def _dft_apply_kernel(a_ref, x_ref, o_ref, xb_scr):
    @pl.when(pl.program_id(2) == 0)
    def _():
        xb_scr[...] = x_ref[...].astype(BF16)

    o_ref[...] = _dot(a_ref[...], xb_scr[...])


def _dft_mul_kernel(a_ref, x_ref, hf_ref, hb_ref, o_ref, xb_scr):
    @pl.when(pl.program_id(2) == 0)
    def _():
        xb_scr[...] = x_ref[...].astype(BF16)

    u = _dot(a_ref[...], xb_scr[...])
    half = u.shape[0] // 2
    ure, uim = u[:half], u[half:]
    hre = hf_ref[0, :half] + hb_ref[0, :half]
    him = hf_ref[0, half:] - hb_ref[0, half:]
    o_ref[:half] = (ure * hre - uim * him).astype(o_ref.dtype)
    o_ref[half:] = (ure * him + uim * hre).astype(o_ref.dtype)


def dft_apply(table, x, n_samples, col0, width, hspec=None, order=0, tn=512):
    m, ls = table.shape
    tm = m // 2
    if hspec is not None:
        tn = 256
    tn = min(tn, width)
    in_specs = [pl.BlockSpec((tm, ls), lambda s, j, i: (i, 0)),
                pl.BlockSpec((ls, tn), lambda s, j, i: (s, col0 // tn + j))]
    args = (table, x)
    if hspec is not None:
        hcol = order * 2 * width // tn
        in_specs += [pl.BlockSpec((1, tm, tn), lambda s, j, i: (0, i, hcol + j)),
                     pl.BlockSpec((1, tm, tn), lambda s, j, i: (0, i, hcol + width // tn + j))]
        args += (hspec, hspec)
    return pl.pallas_call(
        _dft_apply_kernel if hspec is None else _dft_mul_kernel,
        grid=(n_samples, width // tn, m // tm),
        in_specs=in_specs,
        out_specs=pl.BlockSpec((None, tm, tn), lambda s, j, i: (s, i, j)),
        out_shape=jax.ShapeDtypeStruct((n_samples, m, width), F32 if hspec is None else BF16),
        scratch_shapes=[pltpu.VMEM((ls, tn), BF16)],
        compiler_params=_cparams("parallel", "parallel", "arbitrary", vmem_mib=48),
        name="dft_apply" if hspec is None else "dft_mul",
    )(*args)


def _conv_back_kernel(a_ref, y_ref, gate_ref, prev_ref, skip_ref, o_ref):
    y = _dot(a_ref[...], y_ref[...])
    o_ref[...] = (gate_ref[...] * (y + skip_ref[...] * prev_ref[...])).astype(o_ref.dtype)


def conv_back(inv_table, y, gate, gate_col, prev, prev_col, skip, out_dtype, tm=1024, tn=512):
    ns, m2, c = y.shape
    ls = inv_table.shape[0]
    tm = min(tm, ls)
    return pl.pallas_call(
        _conv_back_kernel,
        grid=(ns, c // tn, ls // tm),
        in_specs=[pl.BlockSpec((tm, m2), lambda s, j, i: (i, 0)),
                  pl.BlockSpec((None, m2, tn), lambda s, j, i: (s, 0, j)),
                  pl.BlockSpec((tm, tn), lambda s, j, i: (s * (ls // tm) + i, gate_col // tn + j)),
                  pl.BlockSpec((tm, tn), lambda s, j, i: (s * (ls // tm) + i, prev_col // tn + j)),
                  pl.BlockSpec((1, tn), lambda s, j, i: (0, j))],
        out_specs=pl.BlockSpec((tm, tn), lambda s, j, i: (s * (ls // tm) + i, j)),
        out_shape=jax.ShapeDtypeStruct((ns * ls, c), out_dtype),
        compiler_params=_cparams("parallel", "parallel", "parallel", vmem_mib=48),
        name="conv_back",
    )(inv_table, y, gate, prev, skip)


def hyena_mixer(p, n_samples, conv_w, conv_b, filt_params, skip):
    ls = p.shape[0] // n_samples
    c = HY_W
    fwd, inv = dft_tables(ls)
    filt = hyena_filters_call(ls, *filt_params)
    hspec = dft_apply(fwd, filt, 1, 0, filt.shape[1])
    xc = conv_silu(p, n_samples, 3 * c, conv_w, conv_b, jnp.ones((3 * c,), F32), silu=False)
    v, v_col = xc, 2 * c
    for order in range(HY_ORDER):
        y = dft_apply(fwd, v, n_samples, v_col, c, hspec=hspec, order=order)
        last = order == HY_ORDER - 1
        v = conv_back(inv, y, xc, order * c, v, v_col, skip[order].reshape(1, c), BF16 if last else F32)
        v_col = 0
    return v


def _pad_cols(w, mult=128):
    n = w.shape[-1]
    return jnp.pad(w, ((0, 0), (0, (-n) % mult)))


def kernel(x, c, ctx, c_ctx, ada_w, ada_b, norm1_g, norm2_g, out_w, router_w, moe_w1, moe_w3, moe_w2, final_g, ev_in_w, hy_conv_w, hy_conv_b, hy_pos_w1, hy_pos_b1, hy_sin_freq, hy_pos_w2, hy_pos_b2, hy_pos_w3, hy_bias, gla_gate_w2, gla_gate_b, gla_norm_g, od_in_w, ml_conv_w, ml_conv_b, ml_igate_b, ml_fgate_b, ml_norm_g, da_lambda, da_norm_g):
    B, L, D = x.shape
    Lc = ctx.shape[1]
    depth = ada_w.shape[0]
    cvec = jnp.zeros((N_MOD_ROWS, D), F32).at[:B].set(c).at[B].set(c_ctx)
    mods = ada_mod(cvec, ada_w, ada_b)
    lat_row = lambda s: s
    ctx_row = lambda s: s * 0 + B
    h_x = x.reshape(B * L, D)
    h_c = ctx.reshape(B * Lc, D)
    tm_lat = min(1024, L)
    for l in range(depth):
        last = l == depth - 1
        mod = mods[l].reshape(N_MOD_ROWS, 1, 6 * D)
        g1 = norm1_g[l].reshape(1, D)
        if l % 2 == 0:
            in_w = ev_in_w[l // 2]
        else:
            od = od_in_w[l // 2]
            ml_w = 4 * HALF_W
            in_w = jnp.concatenate([od[:, :ml_w], od[:, ML_IN:], od[:, ml_w:ML_IN]], axis=1)
        w_in = _pad_cols(in_w, 512).astype(BF16)
        w_out = out_w[l].astype(BF16)
        p_lat2 = in_proj(h_x, g1, mod, lat_row, w_in, tm=L)
        p_ctx2 = in_proj(h_c, g1, mod, ctx_row, w_in, tm=B * Lc)
        if l % 2 == 0:
            e = l // 2
            hy_n = 3 * HY_W
            filt_params = (hy_pos_w1[e], hy_pos_b1[e], hy_sin_freq[e], hy_pos_w2[e], hy_pos_b2[e], hy_pos_w3[e])
            ya_l = hyena_mixer(p_lat2, B, hy_conv_w[e], hy_conv_b[e], filt_params, hy_bias[e])
            ya_c = hyena_mixer(p_ctx2, B, hy_conv_w[e], hy_conv_b[e], filt_params, hy_bias[e])
            yb_c, yb_l = gla_mixer(p_ctx2, p_lat2, gla_gate_w2[e], gla_gate_b[e], gla_norm_g[e], B, hy_n)
        else:
            o = l // 2
            lam_init = 0.8 - 0.6 * math.exp(-0.3 * l)
            ya_l = mlstm_mixer(p_ctx2, p_lat2, ml_conv_w[o], ml_conv_b[o], ml_igate_b[o], ml_fgate_b[o], ml_norm_g[o],
                               B, ml_w + 3 * HALF_W)
            yb_l = diff_attention_call(p_ctx2, p_lat2, B, ml_w, da_lambda[o], da_norm_g[o], lam_init)
        h_x = out_proj(ya_l, yb_l, w_out, h_x, mod, tile_rows(lat_row, L, tm_lat), tm=tm_lat)
        streams = [(h_x, lat_row, B)]
        if not last:
            h_c = out_proj(ya_c, yb_c, w_out, h_c, mod, tile_rows(ctx_row, Lc, Lc), tm=Lc)
            streams.append((h_c, ctx_row, B))
        new = moe_layer(streams, norm2_g[l].reshape(1, D), mod, router_w[l], moe_w1, moe_w3, moe_w2, l)
        h_x = new[0]
        if not last:
            h_c = new[1]
    return final_norm(h_x, final_g.reshape(1, D)).reshape(B, L, D)
```

```python
import functools
import math

import jax
import jax.numpy as jnp
import numpy as np
from jax import lax
from jax.experimental import pallas as pl
from jax.experimental.pallas import tpu as pltpu

F32 = jnp.float32
BF16 = jnp.bfloat16
HI = lax.Precision.HIGHEST
NORM_EPS = 1e-6
V7X_VMEM_LIMIT_BYTES = 56 * 1024 * 1024
N_MOD_ROWS = 16
NORM_ROWS = 256
SCAN_PAIR = 2

D_MODEL = 2048
HALF_W = D_MODEL // 2
GRID_W = 64
HY_W = HALF_W
HY_ORDER = 2
HY_BANDS = 16
HY_FAST_DECAY = 0.3
HY_SLOW_DECAY = 1.5
HY_DECAY_TARGET = 1e-2
GLA_HEADS = 4
GLA_DK = HALF_W // (2 * GLA_HEADS)
GLA_DV = HALF_W // GLA_HEADS
GLA_RANK = 16
GLA_TAU = 16.0
GLA_CHUNK = 64
ML_HEADS = 4
ML_DH = HALF_W // ML_HEADS
ML_CHUNK = 256
DA_HEADS = 8
DA_DH = HALF_W // (2 * DA_HEADS)
DA_DV = 2 * DA_DH
Q_BLOCK = 128
ROPE_BASE = 10000.0
N_EXPERTS = 16
EC_CAPACITY = 2
ML_IN = 4 * HALF_W + 4 * ML_HEADS


def _cparams(*sem, vmem_mib=None):
    limit = V7X_VMEM_LIMIT_BYTES if vmem_mib is None else vmem_mib * 1024 * 1024
    return pltpu.CompilerParams(dimension_semantics=sem, vmem_limit_bytes=limit)


def _dot(a, b):
    return jnp.dot(a, b, preferred_element_type=F32)


def _ada_kernel(c_ref, w_ref, b_ref, o_ref):
    c = c_ref[...]
    a = c * jax.nn.sigmoid(c)
    a_hi = a.astype(BF16)
    a_lo = (a - a_hi.astype(F32)).astype(BF16)
    w = w_ref[...]
    w_hi = w.astype(BF16)
    w_lo = (w - w_hi.astype(F32)).astype(BF16)
    n = a.shape[0]
    r1 = _dot(jnp.concatenate([a_hi, a_lo], axis=0), w_hi)
    r2 = _dot(a_hi, w_lo)
    o_ref[...] = r1[:n] + r1[n:] + r2 + b_ref[...]


def ada_mod(cvec, ada_w, ada_b, tn=512):
    n_lyr, d, n = ada_w.shape
    r = cvec.shape[0]
    return pl.pallas_call(
        _ada_kernel,
        grid=(n_lyr, n // tn),
        in_specs=[pl.BlockSpec((r, d), lambda l, j: (0, 0)),
                  pl.BlockSpec((None, d, tn), lambda l, j: (l, 0, j)),
                  pl.BlockSpec((None, 1, tn), lambda l, j: (l, 0, j))],
        out_specs=pl.BlockSpec((None, r, tn), lambda l, j: (l, 0, j)),
        out_shape=jax.ShapeDtypeStruct((n_lyr, r, n), F32),
        compiler_params=_cparams("parallel", "parallel"),
        name="ada_mod",
    )(cvec, ada_w, ada_b.reshape(n_lyr, 1, n))


def _mod_spec(d, chunk, row_fn, ngrid):
    if ngrid == 2:
        return pl.BlockSpec((None, 1, d), lambda i, j: (row_fn(i), 0, chunk))
    return pl.BlockSpec((None, 1, d), lambda i: (row_fn(i), 0, chunk))


def _norm_mod(x, g, shift, scale):
    y = x * lax.rsqrt(jnp.mean(x * x, axis=-1, keepdims=True) + NORM_EPS) * g
    return y * (1.0 + scale) + shift


def _inproj_kernel(h_ref, g_ref, sh_ref, sc_ref, w_ref, o_ref, a_scr):
    @pl.when(pl.program_id(1) == 0)
    def _():
        rows = min(NORM_ROWS, h_ref.shape[0])

        def body(r, carry):
            sl = pl.ds(pl.multiple_of(r * rows, rows), rows)
            a_scr[sl, :] = _norm_mod(h_ref[sl, :], g_ref[...], sh_ref[...], sc_ref[...]).astype(BF16)
            return carry

        lax.fori_loop(0, h_ref.shape[0] // rows, body, 0)

    o_ref[...] = _dot(a_scr[...], w_ref[...])


def in_proj(h, g, mod, row_fn, w, tm=2048, tn=512):
    t, d = h.shape
    n = w.shape[1]
    tm = min(tm, t)
    if n % tn:
        tn = 256 if n % 256 == 0 else 128
    return pl.pallas_call(
        _inproj_kernel,
        grid=(t // tm, n // tn),
        in_specs=[pl.BlockSpec((tm, d), lambda i, j: (i, 0), pipeline_mode=pl.Buffered(1)),
                  pl.BlockSpec((1, d), lambda i, j: (0, 0)),
                  _mod_spec(d, 0, row_fn, 2), _mod_spec(d, 1, row_fn, 2),
                  pl.BlockSpec((d, tn), lambda i, j: (0, j))],
        out_specs=pl.BlockSpec((tm, tn), lambda i, j: (i, j)),
        out_shape=jax.ShapeDtypeStruct((t, n), F32),
        scratch_shapes=[pltpu.VMEM((tm, d), BF16)],
        compiler_params=_cparams("parallel", "arbitrary", vmem_mib=48),
        name="in_proj",
    )(h, g, mod, mod, w)


def _outproj_kernel(ya_ref, yb_ref, w_ref, h_ref, gate_ref, o_ref):
    ka = ya_ref.shape[1]
    acc = _dot(ya_ref[...], w_ref[:ka, :]) + _dot(yb_ref[...], w_ref[ka:, :])
    o_ref[...] = h_ref[...] + gate_ref[...] * acc


def out_proj(ya, yb, w, h, mod, row_fn, tm=1024, tn=512):
    t, d = h.shape
    ka, kb = ya.shape[1], yb.shape[1]
    tm, tn = min(tm, t), min(tn, d)
    return pl.pallas_call(
        _outproj_kernel,
        grid=(t // tm, d // tn),
        in_specs=[pl.BlockSpec((tm, ka), lambda i, j: (i, 0)),
                  pl.BlockSpec((tm, kb), lambda i, j: (i, 0)),
                  pl.BlockSpec((ka + kb, tn), lambda i, j: (0, j)),
                  pl.BlockSpec((tm, tn), lambda i, j: (i, j)),
                  pl.BlockSpec((None, 1, tn), lambda i, j: (row_fn(i), 0, 2 * (d // tn) + j))],
        out_specs=pl.BlockSpec((tm, tn), lambda i, j: (i, j)),
        out_shape=jax.ShapeDtypeStruct((t, d), F32),
        compiler_params=_cparams("parallel", "parallel", vmem_mib=32),
        name="out_proj",
    )(ya, yb, w, h, mod)


def _moeprep_kernel(h_ref, g_ref, sh_ref, sc_ref, rw_ref, a_ref, lg_ref):
    a = _norm_mod(h_ref[...], g_ref[...], sh_ref[...], sc_ref[...])
    a_ref[...] = a.astype(BF16)
    lg_ref[...] = jnp.dot(a, rw_ref[...], precision=HI, preferred_element_type=F32)


def moe_prep(h, g, mod, row_fn, router_w, tm=512):
    t, d = h.shape
    e = router_w.shape[1]
    tm = min(tm, t)
    return pl.pallas_call(
        _moeprep_kernel,
        grid=(t // tm,),
        in_specs=[pl.BlockSpec((tm, d), lambda i: (i, 0)),
                  pl.BlockSpec((1, d), lambda i: (0, 0)),
                  _mod_spec(d, 3, row_fn, 1), _mod_spec(d, 4, row_fn, 1),
                  pl.BlockSpec((d, e), lambda i: (0, 0))],
        out_specs=[pl.BlockSpec((tm, d), lambda i: (i, 0)), pl.BlockSpec((tm, e), lambda i: (i, 0))],
        out_shape=[jax.ShapeDtypeStruct((t, d), BF16), jax.ShapeDtypeStruct((t, e), F32)],
        compiler_params=_cparams("parallel"),
        name="moe_prep",
    )(h, g, mod, mod, router_w)


def _route_kernel(lg_ref, pos_ref, gate_ref, *, cap):
    lg = lg_ref[...]
    ts, ne = lg.shape
    ex = jnp.exp(lg - jnp.max(lg, axis=-1, keepdims=True))
    aff = ex / jnp.sum(ex, axis=-1, keepdims=True)
    bits = pltpu.bitcast(aff, jnp.int32)

    def bisect(i, thr):
        cand = thr | jnp.left_shift(jnp.int32(1), 30 - i)
        cnt = jnp.sum((bits >= cand).astype(F32), axis=0, keepdims=True)
        return jnp.where(cnt >= cap, cand, thr)

    thr = lax.fori_loop(0, 31, bisect, jnp.zeros((1, ne), jnp.int32))
    gt = bits > thr
    eq = bits == thr
    n_gt = jnp.sum(gt.astype(F32), axis=0, keepdims=True)
    r = lax.broadcasted_iota(jnp.int32, (ts, ts), 0)
    c = lax.broadcasted_iota(jnp.int32, (ts, ts), 1)
    tri = (c < r).astype(BF16)
    eq_rank = _dot(tri, eq.astype(BF16))
    sel = gt | (eq & (eq_rank < cap - n_gt))
    pos = _dot(tri, sel.astype(BF16))
    pos_ref[...] = jnp.where(sel, pos, -1.0)
    gate_ref[...] = jnp.where(sel, aff, 0.0)


def moe_route(logits, n_samples, cap):
    t, e = logits.shape
    ts = t // n_samples
    spec = pl.BlockSpec((ts, e), lambda s: (s, 0))
    return pl.pallas_call(
        functools.partial(_route_kernel, cap=cap),
        grid=(n_samples,),
        in_specs=[spec],
        out_specs=[spec, spec],
        out_shape=[jax.ShapeDtypeStruct((t, e), F32)] * 2,
        compiler_params=_cparams("parallel"),
        name="moe_route",
    )(logits)


def _slot_onehot(pos_col, cap):
    slots = lax.broadcasted_iota(jnp.int32, (1, cap), 1).astype(F32)
    return (pos_col == slots).astype(BF16)


def _gather_kernel(pos_ref, a_ref, x_ref, *, cap):
    pos = pos_ref[...]
    a = a_ref[...]
    for e in range(pos.shape[1]):
        pt = _slot_onehot(pos[:, e:e + 1], cap)
        x = lax.dot_general(pt, a, (((0,), (0,)), ((), ())), preferred_element_type=F32)
        x_ref[e] = x.astype(BF16)


def moe_gather(pos, a, n_samples, cap, tn=1024):
    t, d = a.shape
    e = pos.shape[1]
    ts = t // n_samples
    tn = min(tn, d)
    return pl.pallas_call(
        functools.partial(_gather_kernel, cap=cap),
        grid=(n_samples, d // tn),
        in_specs=[pl.BlockSpec((ts, e), lambda s, j: (s, 0)), pl.BlockSpec((ts, tn), lambda s, j: (s, j))],
        out_specs=pl.BlockSpec((e, cap, tn), lambda s, j: (0, s, j)),
        out_shape=jax.ShapeDtypeStruct((e, n_samples * cap, d), BF16),
        compiler_params=_cparams("parallel", "parallel"),
        name="moe_gather",
    )(pos, a)


def _ffn_kernel(*refs, nx, nff):
    x_refs = refs[:nx]
    w1_ref, w3_ref, w2_ref = refs[nx:nx + 3]
    y_refs = refs[nx + 3:2 * nx + 3]
    hid_refs = refs[2 * nx + 3:]
    j = pl.program_id(1)

    @pl.when(j < nff)
    def _():
        w1 = w1_ref[...].astype(BF16)
        w3 = w3_ref[...].astype(BF16)
        for x_ref, hid_ref in zip(x_refs, hid_refs):
            x = x_ref[...]
            h1 = _dot(x, w1)
            h3 = _dot(x, w3)
            hid_ref[j] = (h1 * jax.nn.sigmoid(h1) * h3).astype(BF16)

    @pl.when(j >= nff)
    def _():
        w2 = w2_ref[...].astype(BF16)
        tf = w2.shape[0] // nff
        for y_ref, hid_ref in zip(y_refs, hid_refs):
            acc = _dot(hid_ref[0], w2[:tf])
            for jj in range(1, nff):
                acc += _dot(hid_ref[jj], w2[jj * tf:(jj + 1) * tf])
            y_ref[...] = acc.astype(y_ref.dtype)


def moe_ffn(xs, w1, w3, w2, layer, tf=256, tn=256):
    _, ne, d, ff = w1.shape
    tf, tn = min(tf, ff), min(tn, d)
    nff, nd = ff // tf, d // tn
    nx = len(xs)
    x_specs = [pl.BlockSpec((None, x.shape[1], d), lambda e, j: (e, 0, 0)) for x in xs]
    up_spec = pl.BlockSpec((None, None, d, tf), lambda e, j: (layer, e, 0, jnp.minimum(j, nff - 1)))
    down_spec = pl.BlockSpec((None, None, ff, tn), lambda e, j: (layer, e, 0, jnp.maximum(j - nff, 0)))
    y_specs = [pl.BlockSpec((None, x.shape[1], tn), lambda e, j: (e, 0, jnp.maximum(j - nff, 0))) for x in xs]
    return pl.pallas_call(
        functools.partial(_ffn_kernel, nx=nx, nff=nff),
        grid=(ne, nff + nd),
        in_specs=x_specs + [up_spec, up_spec, down_spec],
        out_specs=y_specs,
        out_shape=[jax.ShapeDtypeStruct(x.shape, BF16) for x in xs],
        scratch_shapes=[pltpu.VMEM((nff, x.shape[1], tf), BF16) for x in xs],
        compiler_params=_cparams("parallel", "arbitrary"),
        name="moe_ffn",
    )(*xs, w1, w3, w2)


def _combine_kernel(pos_ref, gate_ref, y_ref, h_ref, m_ref, o_ref, pt_scr, *, cap):
    ne = pos_ref.shape[1]

    @pl.when(pl.program_id(1) == 0)
    def _():
        pos = pos_ref[...]
        for e in range(ne):
            pt_scr[e] = _slot_onehot(pos[:, e:e + 1], cap)

    gate = gate_ref[...]
    acc = gate[:, 0:1] * _dot(pt_scr[0], y_ref[0])
    for e in range(1, ne):
        acc += gate[:, e:e + 1] * _dot(pt_scr[e], y_ref[e])
    o_ref[...] = h_ref[...] + m_ref[...] * acc


def moe_combine(pos, gate, y, h, mod, row_fn, n_samples, cap, tn=512):
    t, d = h.shape
    e = pos.shape[1]
    ts = t // n_samples
    tn = min(tn, d)
    return pl.pallas_call(
        functools.partial(_combine_kernel, cap=cap),
        grid=(n_samples, d // tn),
        in_specs=[pl.BlockSpec((ts, e), lambda s, j: (s, 0)),
                  pl.BlockSpec((ts, e), lambda s, j: (s, 0)),
                  pl.BlockSpec((e, cap, tn), lambda s, j: (0, s, j)),
                  pl.BlockSpec((ts, tn), lambda s, j: (s, j)),
                  pl.BlockSpec((None, 1, tn), lambda s, j: (row_fn(s), 0, 5 * (d // tn) + j))],
        out_specs=pl.BlockSpec((ts, tn), lambda s, j: (s, j)),
        out_shape=jax.ShapeDtypeStruct((t, d), F32),
        scratch_shapes=[pltpu.VMEM((e, ts, cap), BF16)],
        compiler_params=_cparams("parallel", "arbitrary"),
        name="moe_combine",
    )(pos, gate, y, h, mod)


def _rms_kernel(x_ref, g_ref, o_ref):
    x = x_ref[...]
    o_ref[...] = x * lax.rsqrt(jnp.mean(x * x, axis=-1, keepdims=True) + NORM_EPS) * g_ref[...]


def final_norm(h, g, tm=512):
    t, d = h.shape
    return pl.pallas_call(
        _rms_kernel,
        grid=(t // tm,),
        in_specs=[pl.BlockSpec((tm, d), lambda i: (i, 0)), pl.BlockSpec((1, d), lambda i: (0, 0))],
        out_specs=pl.BlockSpec((tm, d), lambda i: (i, 0)),
        out_shape=jax.ShapeDtypeStruct((t, d), F32),
        compiler_params=_cparams("parallel"),
        name="final_norm",
    )(h, g)


def tile_rows(row_of_sample, ts, tm):
    per = ts // tm
    return lambda i: row_of_sample(i // per)


def moe_layer(streams, g2, mod, router_w, w1, w3, w2, layer):
    routed = []
    for h, row_of_sample, n_samples in streams:
        ts = h.shape[0] // n_samples
        cap = EC_CAPACITY * ts // N_EXPERTS
        tm = min(512, ts)
        a, logits = moe_prep(h, g2, mod, tile_rows(row_of_sample, ts, tm), router_w, tm=tm)
        pos, gate = moe_route(logits, n_samples, cap)
        routed.append((pos, gate, cap, moe_gather(pos, a, n_samples, cap)))
    ys = moe_ffn([r[3] for r in routed], w1, w3, w2, layer)
    return [moe_combine(pos, gate, y, h, mod, row_of_sample, n_samples, cap)
            for (h, row_of_sample, n_samples), (pos, gate, cap, _), y in zip(streams, routed, ys)]


def _dir_tri(n, d):
    i = lax.broadcasted_iota(jnp.int32, (n, n), 0)
    j = lax.broadcasted_iota(jnp.int32, (n, n), 1)
    return (j - i) * (1 - 2 * d) <= 0


def _gla_chunk(q_ref, k_ref, v_ref, c_ref, w2_ref, gb_ref, o_ref, st_ref, d):
    cs = q_ref.shape[0]
    causal = _dir_tri(cs, d)
    logits = jnp.dot(c_ref[...], w2_ref[...], precision=HI, preferred_element_type=F32) + gb_ref[...]
    logg = jax.nn.log_sigmoid(logits) * (1.0 / GLA_TAU)
    b = jnp.dot(causal.astype(F32), logg, precision=HI, preferred_element_type=F32)
    b_last = jnp.sum(logg, axis=0, keepdims=True)
    q = q_ref[...] * (GLA_DK ** -0.5)
    k = k_ref[...]
    q_dec = (q * jnp.exp(b)).astype(BF16)
    k_dec = (k * jnp.exp(-b)).astype(BF16)
    k_end = (k * jnp.exp(b_last - b)).astype(BF16)
    v = v_ref[...].astype(BF16)
    for h in range(GLA_HEADS):
        ks = slice(h * GLA_DK, (h + 1) * GLA_DK)
        vs = slice(h * GLA_DV, (h + 1) * GLA_DV)
        st = st_ref[h]
        att = lax.dot_general(q_dec[:, ks], k_dec[:, ks], (((1,), (1,)), ((), ())), preferred_element_type=F32)
        att = jnp.where(causal, att, 0.0).astype(BF16)
        o = _dot(att, v[:, vs]) + lax.dot_general(q_dec[:, ks], st.astype(BF16), (((1,), (1,)), ((), ())),
                                                  preferred_element_type=F32)
        o_ref[:, vs] = o
        upd = lax.dot_general(v[:, vs], k_end[:, ks], (((0,), (0,)), ((), ())), preferred_element_type=F32)
        st_ref[h] = st * jnp.exp(b_last[:, ks]) + upd


def _gla_kernel(qc_ref, kc_ref, vc_ref, cc_ref, ql_ref, kl_ref, vl_ref, cl_ref, w2_ref, gb_ref,
                oc_ref, ol_ref, st_ref, *, ncc):
    d = pl.program_id(1)
    c = pl.program_id(2)

    @pl.when(c == 0)
    def _():
        st_ref[...] = jnp.zeros_like(st_ref)

    @pl.when(c < ncc)
    def _():
        for s in range(qc_ref.shape[0]):
            _gla_chunk(qc_ref.at[s], kc_ref.at[s], vc_ref.at[s], cc_ref.at[s], w2_ref, gb_ref, oc_ref.at[s],
                       st_ref.at[s], d)

    @pl.when(c >= ncc)
    def _():
        for s in range(ql_ref.shape[0]):
            _gla_chunk(ql_ref.at[s], kl_ref.at[s], vl_ref.at[s], cl_ref.at[s], w2_ref, gb_ref, ol_ref.at[s],
                       st_ref.at[s], d)


def _scan_chunk_maps(n_samples_chunks_ctx, n_samples_chunks_lat):
    ncc, nlc = n_samples_chunks_ctx, n_samples_chunks_lat

    def ctx_blk(b, d, c):
        i = jnp.minimum(c, ncc - 1)
        return b * ncc + jnp.where(d == 0, i, ncc - 1 - i)

    def lat_blk(b, d, c):
        i = jnp.maximum(c - ncc, 0)
        return b * nlc + jnp.where(d == 0, i, nlc - 1 - i)

    return ctx_blk, lat_blk


def gla_scan_call(p_ctx, p_lat, w2_full, gate_b, n_samples, col0, col_codes):
    cs = GLA_CHUNK
    hk, hv = GLA_HEADS * GLA_DK, GLA_HEADS * GLA_DV
    ncc, nlc = p_ctx.shape[0] // n_samples // cs, p_lat.shape[0] // n_samples // cs
    ctx_blk, lat_blk = _scan_chunk_maps(ncc, nlc)
    qb, kb, vb = col0 // hk, col0 // hk + 1, (col0 + 2 * hk) // hv
    cb = col_codes // LANES
    pair = SCAN_PAIR if n_samples % SCAN_PAIR == 0 else 1
    p_ctx = p_ctx.reshape(pair, p_ctx.shape[0] // pair, p_ctx.shape[1])
    p_lat = p_lat.reshape(pair, p_lat.shape[0] // pair, p_lat.shape[1])

    def specs(blk):
        return [pl.BlockSpec((pair, cs, hk), lambda b, d, c: (0, blk(b, d, c), qb)),
                pl.BlockSpec((pair, cs, hk), lambda b, d, c: (0, blk(b, d, c), kb)),
                pl.BlockSpec((pair, cs, hv), lambda b, d, c: (0, blk(b, d, c), vb)),
                pl.BlockSpec((pair, cs, 128), lambda b, d, c: (0, blk(b, d, c), cb))]

    o_ctx, o_lat = pl.pallas_call(
        functools.partial(_gla_kernel, ncc=ncc),
        grid=(n_samples // pair, 2, ncc + nlc),
        in_specs=specs(ctx_blk) + specs(lat_blk) + [
            pl.BlockSpec((None, 128, hk), lambda b, d, c: (d, 0, 0)),
            pl.BlockSpec((None, 1, hk), lambda b, d, c: (d, 0, 0))],
        out_specs=[pl.BlockSpec((None, pair, cs, hv), lambda b, d, c: (d, 0, ctx_blk(b, d, c), 0)),
                   pl.BlockSpec((None, pair, cs, hv), lambda b, d, c: (d, 0, lat_blk(b, d, c), 0))],
        out_shape=[jax.ShapeDtypeStruct((2,) + p_ctx.shape[:2] + (hv,), F32),
                   jax.ShapeDtypeStruct((2,) + p_lat.shape[:2] + (hv,), F32)],
        scratch_shapes=[pltpu.VMEM((pair, GLA_HEADS, GLA_DV, GLA_DK), F32)],
        compiler_params=_cparams("parallel", "arbitrary", "arbitrary", vmem_mib=32),
        name="gla_scan",
    )(p_ctx, p_ctx, p_ctx, p_ctx, p_lat, p_lat, p_lat, p_lat, w2_full, gate_b)
    return o_ctx.reshape(2, -1, hv), o_lat.reshape(2, -1, hv)


def _headnorm_gate_kernel(o_ref, r_ref, g_ref, y_ref, *, n_heads, act):
    o = o_ref[0] + o_ref[1]
    r = r_ref[...]
    gate = r * jax.nn.sigmoid(r) if act == "silu" else jax.nn.sigmoid(r)
    dh = o.shape[1] // n_heads
    for h in range(n_heads):
        s = slice(h * dh, (h + 1) * dh)
        oh = o[:, s]
        yh = oh * lax.rsqrt(jnp.mean(oh * oh, axis=-1, keepdims=True) + NORM_EPS) * g_ref[...]
        y_ref[:, s] = (yh * gate[:, s]).astype(y_ref.dtype)


def headnorm_gate(o2, p, r_col, norm_g, n_heads, act, tm=512):
    _, t, w = o2.shape
    tm = min(tm, t)
    return pl.pallas_call(
        functools.partial(_headnorm_gate_kernel, n_heads=n_heads, act=act),
        grid=(t // tm,),
        in_specs=[pl.BlockSpec((2, tm, w), lambda i: (0, i, 0)),
                  pl.BlockSpec((tm, w), lambda i: (i, r_col // w)),
                  pl.BlockSpec((1, w // n_heads), lambda i: (0, 0))],
        out_specs=pl.BlockSpec((tm, w), lambda i: (i, 0)),
        out_shape=jax.ShapeDtypeStruct((t, w), BF16),
        compiler_params=_cparams("parallel", vmem_mib=32),
        name="headnorm_gate",
    )(o2, p, norm_g.reshape(1, -1))


def gla_mixer(p_ctx, p_lat, gate_w2, gate_b, norm_g, n_samples, col0):
    hk, hv = GLA_HEADS * GLA_DK, GLA_HEADS * GLA_DV
    w2_full = jnp.zeros((2, 128, hk), F32)
    for d in range(2):
        w2_full = w2_full.at[d, d * GLA_RANK:(d + 1) * GLA_RANK].set(gate_w2[d])
    r_col = col0 + 2 * hk + hv
    o_ctx, o_lat = gla_scan_call(p_ctx, p_lat, w2_full, gate_b.reshape(2, 1, hk), n_samples, col0, r_col + hv)
    return (headnorm_gate(o_ctx, p_ctx, r_col, norm_g, GLA_HEADS, "silu"),
            headnorm_gate(o_lat, p_lat, r_col, norm_g, GLA_HEADS, "silu"))


ROPE_NF = DA_DH // 4
LANES = 128


def _rope_table_kernel(cos_ref, sin_ref):
    n = cos_ref.shape[0]
    t = lax.broadcasted_iota(jnp.int32, (n, LANES), 0)
    j = lax.broadcasted_iota(jnp.int32, (n, LANES), 1)
    dd = j % DA_DH
    is_col = (dd // (2 * ROPE_NF)) == 1
    is_xb = ((dd // ROPE_NF) % 2) == 1
    f = (dd % ROPE_NF).astype(F32)
    inv = jnp.exp(f * (-math.log(ROPE_BASE) / ROPE_NF))
    pos = jnp.where(is_col, t % GRID_W, t // GRID_W).astype(F32)
    ang = pos * inv
    cos_ref[...] = jnp.cos(ang)
    sin_ref[...] = jnp.where(is_xb, 1.0, -1.0) * jnp.sin(ang)


def rope_tables(n):
    out = jax.ShapeDtypeStruct((n, LANES), F32)
    return pl.pallas_call(_rope_table_kernel, out_shape=[out, out], name="rope_tables")()


def _rope(x, cos, sin_signed):
    lane = lax.broadcasted_iota(jnp.int32, x.shape, 1)
    is_xb = ((lane // ROPE_NF) % 2) == 1
    partner = jnp.where(is_xb, pltpu.roll(x, ROPE_NF, 1), pltpu.roll(x, LANES - ROPE_NF, 1))
    return x * cos + partner * sin_signed


def _nt(a, b):
    return lax.dot_general(a, b, (((1,), (1,)), ((), ())), preferred_element_type=F32)


def _diffattn_kernel(q_ref, kl_ref, vl_ref, kc_ref, vc_ref, cq_ref, sq_ref, ck_ref, sk_ref, lam_ref, g_ref, o_ref,
                     kl_scr, vl_scr, kc_scr, vc_scr, *, lam_init):
    @pl.when(pl.program_id(2) == 0)
    def _():
        kl_scr[...] = _rope(kl_ref[...], ck_ref[...], sk_ref[...]).astype(BF16)
        kc_scr[...] = kc_ref[...].astype(BF16)
        vl_scr[:, :LANES] = vl_ref[...].astype(BF16)
        vl_scr[:, LANES:] = jnp.ones(vl_ref.shape, BF16)
        vc_scr[:, :LANES] = vc_ref[...].astype(BF16)
        vc_scr[:, LANES:] = jnp.ones(vc_ref.shape, BF16)

    lv = lam_ref[...]
    lam = (jnp.exp(jnp.sum(lv[0:1] * lv[1:2], keepdims=True)) - jnp.exp(jnp.sum(lv[2:3] * lv[3:4], keepdims=True))
           + lam_init)
    q = _rope(q_ref[...], cq_ref[...], sq_ref[...]) * (DA_DH ** -0.5)
    first = lax.broadcasted_iota(jnp.int32, q.shape, 1) < DA_DH
    attn = []
    for comp in range(2):
        qc = jnp.where(first == (comp == 0), q, 0.0).astype(BF16)
        s_l = _nt(qc, kl_scr[...])
        s_c = _nt(qc, kc_scr[...])
        m = jnp.maximum(jnp.max(s_l, axis=-1, keepdims=True), jnp.max(s_c, axis=-1, keepdims=True))
        acc = _dot(jnp.exp(s_l - m).astype(BF16), vl_scr[...]) + _dot(jnp.exp(s_c - m).astype(BF16), vc_scr[...])
        attn.append(acc[:, :LANES] / acc[:, LANES:])
    o = attn[0] - lam * attn[1]
    o = o * lax.rsqrt(jnp.mean(o * o, axis=-1, keepdims=True) + NORM_EPS) * g_ref[...]
    o_ref[...] = (o * (1.0 - lam_init)).astype(o_ref.dtype)


def diff_attention_call(p_ctx, p_lat, n_samples, col_q, lam_vecs, norm_g, lam_init, tq=512):
    lt, lc = p_lat.shape[0] // n_samples, p_ctx.shape[0] // n_samples
    tq = min(tq, lt)
    nq = lt // tq
    w = DA_HEADS * DA_DV
    qb, kb, vb = col_q // LANES, (col_q + w) // LANES, (col_q + 2 * w) // LANES
    cos, sin = rope_tables(lt)
    return pl.pallas_call(
        functools.partial(_diffattn_kernel, lam_init=lam_init),
        grid=(n_samples, DA_HEADS, nq),
        in_specs=[pl.BlockSpec((tq, LANES), lambda b, h, i: (b * nq + i, qb + h)),
                  pl.BlockSpec((lt, LANES), lambda b, h, i: (b, kb + h)),
                  pl.BlockSpec((lt, LANES), lambda b, h, i: (b, vb + h)),
                  pl.BlockSpec((lc, LANES), lambda b, h, i: (b, kb + h)),
                  pl.BlockSpec((lc, LANES), lambda b, h, i: (b, vb + h)),
                  pl.BlockSpec((tq, LANES), lambda b, h, i: (i, 0)),
                  pl.BlockSpec((tq, LANES), lambda b, h, i: (i, 0)),
                  pl.BlockSpec((lt, LANES), lambda b, h, i: (0, 0)),
                  pl.BlockSpec((lt, LANES), lambda b, h, i: (0, 0)),
                  pl.BlockSpec(lam_vecs.shape, lambda b, h, i: (0, 0)),
                  pl.BlockSpec((1, DA_DV), lambda b, h, i: (0, 0))],
        out_specs=pl.BlockSpec((tq, LANES), lambda b, h, i: (b * nq + i, h)),
        out_shape=jax.ShapeDtypeStruct((p_lat.shape[0], w), BF16),
        scratch_shapes=[pltpu.VMEM((lt, LANES), BF16), pltpu.VMEM((lt, 2 * LANES), BF16),
                        pltpu.VMEM((lc, LANES), BF16), pltpu.VMEM((lc, 2 * LANES), BF16)],
        compiler_params=_cparams("parallel", "parallel", "arbitrary", vmem_mib=48),
        name="diff_attention",
    )(p_lat, p_lat, p_lat, p_ctx, p_ctx, cos, sin, cos, sin, lam_vecs, norm_g.reshape(1, DA_DV))


def _convsilu_kernel(u_ref, w_ref, b_ref, s_ref, o_ref, *, silu):
    u = u_ref[...]
    n = u.shape[0]
    row = lax.broadcasted_iota(jnp.int32, u.shape, 0)
    prev = jnp.where(row == 0, 0.0, pltpu.roll(u, 1, 0))
    nxt = jnp.where(row == n - 1, 0.0, pltpu.roll(u, n - 1, 0))
    w = w_ref[...]
    y = w[0:1] * prev + w[1:2] * u + w[2:3] * nxt + b_ref[...]
    if silu:
        y = y * jax.nn.sigmoid(y)
    o_ref[...] = y * s_ref[...]


def conv_silu(p, n_samples, width, conv_w, conv_b, col_scale, tn=512, silu=True):
    ls = p.shape[0] // n_samples
    return pl.pallas_call(
        functools.partial(_convsilu_kernel, silu=silu),
        grid=(n_samples, width // tn),
        in_specs=[pl.BlockSpec((ls, tn), lambda s, j: (s, j)),
                  pl.BlockSpec((conv_w.shape[0], tn), lambda s, j: (0, j)),
                  pl.BlockSpec((1, tn), lambda s, j: (0, j)),
                  pl.BlockSpec((1, tn), lambda s, j: (0, j))],
        out_specs=pl.BlockSpec((ls, tn), lambda s, j: (s, j)),
        out_shape=jax.ShapeDtypeStruct((p.shape[0], width), F32),
        compiler_params=_cparams("parallel", "parallel", vmem_mib=32),
        name="conv_silu",
    )(p, conv_w, conv_b.reshape(1, -1), col_scale.reshape(1, -1))


def _mlstm_chunk(q_ref, k_ref, v_ref, g_ref, gb_ref, o_ref, c_ref, n_ref, m_ref, d, emit):
    cs = q_ref.shape[0]
    causal = _dir_tri(cs, d)
    g = g_ref[...] + gb_ref[...]
    lane = lax.broadcasted_iota(jnp.int32, g.shape, 1)
    gl = jnp.where((lane // ML_HEADS) % 2 == 1, jax.nn.log_sigmoid(g), g)
    bcum = jnp.dot(causal.astype(F32), gl, precision=HI, preferred_element_type=F32)
    gl_t = gl.T
    bcum_t = bcum.T
    q = q_ref[...].astype(BF16)
    k = k_ref[...]
    kb = k.astype(BF16)
    v = v_ref[...]

    def pick_col(a, idx):
        return jnp.where(d == 0, a[:, idx:idx + 1], a[:, 2 * ML_HEADS + idx:2 * ML_HEADS + idx + 1])

    def pick_row(a, idx):
        return jnp.where(d == 0, a[idx:idx + 1, :], a[2 * ML_HEADS + idx:2 * ML_HEADS + idx + 1, :])

    for h in range(ML_HEADS):
        hs = slice(h * ML_DH, (h + 1) * ML_DH)
        ic_c, ic_r = pick_col(gl, h), pick_row(gl_t, h)
        fc_c = pick_col(gl, ML_HEADS + h)
        b_c, b_r = pick_col(bcum, ML_HEADS + h), pick_row(bcum_t, ML_HEADS + h)
        b_last = jnp.sum(fc_c, axis=0, keepdims=True)
        m = m_ref[h][:, 0:1]
        cm = c_ref[h]
        nv = n_ref[h]
        w_end_c = b_last - b_c + ic_c
        w_end_r = b_last - b_r + ic_r
        m_new = jnp.maximum(b_last + m, jnp.max(w_end_r, axis=-1, keepdims=True))
        keep = jnp.exp(b_last + m - m_new)
        w_c = jnp.exp(w_end_c - m_new)
        vw = (v[:, hs] * w_c).astype(BF16)
        c_ref[h] = keep * cm + lax.dot_general(vw, kb[:, hs], (((0,), (0,)), ((), ())), preferred_element_type=F32)
        n_ref[h] = keep * nv + jnp.sum(w_c * k[:, hs], axis=0, keepdims=True)
        m_ref[h] = jnp.broadcast_to(m_new, m_ref.shape[1:])
        if emit:
            a_c = b_c + m
            dlog = jnp.where(causal, b_c - b_r + ic_r, -jnp.inf)
            m_t = jnp.maximum(a_c, jnp.max(dlog, axis=-1, keepdims=True))
            sc = _nt(q[:, hs], kb[:, hs]) * jnp.exp(dlog - m_t)
            aw = jnp.exp(a_c - m_t)
            num = _dot(sc.astype(BF16), v[:, hs].astype(BF16)) + aw * _nt(q[:, hs], cm.astype(BF16))
            qf = q_ref[:, hs]
            den = jnp.sum(sc, axis=-1, keepdims=True) + aw * jnp.sum(qf * nv, axis=-1, keepdims=True)
            o_ref[:, hs] = num / jnp.maximum(jnp.abs(den), jnp.exp(-m_t))


def _mlstm_kernel(qc_ref, kc_ref, vc_ref, gc_ref, ql_ref, kl_ref, vl_ref, gl_ref, gb_ref,
                  ol_ref, c_ref, n_ref, m_ref, *, ncc):
    d = pl.program_id(1)
    c = pl.program_id(2)

    @pl.when(c == 0)
    def _():
        c_ref[...] = jnp.zeros_like(c_ref)
        n_ref[...] = jnp.zeros_like(n_ref)
        m_ref[...] = jnp.zeros_like(m_ref)

    @pl.when(c < ncc)
    def _():
        for s in range(qc_ref.shape[0]):
            _mlstm_chunk(qc_ref.at[s], kc_ref.at[s], vc_ref.at[s], gc_ref.at[s], gb_ref, None,
                         c_ref.at[s], n_ref.at[s], m_ref.at[s], d, False)

    @pl.when(c >= ncc)
    def _():
        for s in range(ql_ref.shape[0]):
            _mlstm_chunk(ql_ref.at[s], kl_ref.at[s], vl_ref.at[s], gl_ref.at[s], gb_ref, ol_ref.at[s],
                         c_ref.at[s], n_ref.at[s], m_ref.at[s], d, True)


def mlstm_scan_call(qk_ctx, qk_lat, p_ctx, p_lat, gate_bias, n_samples, col_v, col_g):
    cs = ML_CHUNK
    w = ML_HEADS * ML_DH
    ncc, nlc = p_ctx.shape[0] // n_samples // cs, p_lat.shape[0] // n_samples // cs
    ctx_blk, lat_blk = _scan_chunk_maps(ncc, nlc)
    vb = col_v // w
    gcol = col_g // LANES
    pair = SCAN_PAIR if n_samples % SCAN_PAIR == 0 else 1
    halves = lambda a: a.reshape(pair, a.shape[0] // pair, a.shape[1])
    qk_ctx, qk_lat, p_ctx, p_lat = halves(qk_ctx), halves(qk_lat), halves(p_ctx), halves(p_lat)

    def specs(blk):
        return [pl.BlockSpec((pair, cs, w), lambda b, d, c: (0, blk(b, d, c), 0)),
                pl.BlockSpec((pair, cs, w), lambda b, d, c: (0, blk(b, d, c), 1)),
                pl.BlockSpec((pair, cs, w), lambda b, d, c: (0, blk(b, d, c), vb)),
                pl.BlockSpec((pair, cs, LANES), lambda b, d, c: (0, blk(b, d, c), gcol))]

    out = pl.pallas_call(
        functools.partial(_mlstm_kernel, ncc=ncc),
        grid=(n_samples // pair, 2, ncc + nlc),
        in_specs=specs(ctx_blk) + specs(lat_blk) + [pl.BlockSpec((1, LANES), lambda b, d, c: (0, 0))],
        out_specs=pl.BlockSpec((None, pair, cs, w), lambda b, d, c: (d, 0, lat_blk(b, d, c), 0)),
        out_shape=jax.ShapeDtypeStruct((2,) + p_lat.shape[:2] + (w,), F32),
        scratch_shapes=[pltpu.VMEM((pair, ML_HEADS, ML_DH, ML_DH), F32), pltpu.VMEM((pair, ML_HEADS, 1, ML_DH), F32),
                        pltpu.VMEM((pair, ML_HEADS, 1, LANES), F32)],
        compiler_params=_cparams("parallel", "arbitrary", "arbitrary", vmem_mib=32),
        name="mlstm_scan",
    )(qk_ctx, qk_ctx, p_ctx, p_ctx, qk_lat, qk_lat, p_lat, p_lat, gate_bias)
    return out.reshape(2, -1, w)


def mlstm_mixer(p_ctx, p_lat, conv_w, conv_b, igate_b, fgate_b, norm_g, n_samples, col_g):
    w = ML_HEADS * ML_DH
    scale = jnp.concatenate([jnp.ones((w,), F32), jnp.full((w,), ML_DH ** -0.5, F32)])
    qk_ctx = conv_silu(p_ctx, n_samples, 2 * w, conv_w, conv_b, scale)
    qk_lat = conv_silu(p_lat, n_samples, 2 * w, conv_w, conv_b, scale)
    bias = jnp.concatenate([jnp.stack([igate_b[d], fgate_b[d]]).reshape(-1) for d in range(2)])
    gate_bias = jnp.zeros((1, LANES), F32).at[0, :bias.shape[0]].set(bias)
    o2 = mlstm_scan_call(qk_ctx, qk_lat, p_ctx, p_lat, gate_bias, n_samples, 2 * w, col_g)
    return headnorm_gate(o2, p_lat, 3 * w, norm_g, ML_HEADS, "sigmoid")


HY_FEAT_PAD = 128
DFT_FREQ_PAD = 128


def _hyfilt_kernel(w1_ref, b1_ref, fr_ref, w2_ref, b2_ref, w3_ref, o_ref, *, seq):
    tn = o_ref.shape[1]
    pos = lax.broadcasted_iota(jnp.int32, (seq, HY_FEAT_PAD), 0).astype(F32)
    lane = lax.broadcasted_iota(jnp.int32, (seq, HY_FEAT_PAD), 1)
    t = pos / max(seq - 1, 1)
    band = 1e-4 + ((lane - 1) % HY_BANDS).astype(F32) * ((HY_BANDS - 1 - 1e-4) / (HY_BANDS - 1))
    ang = (2.0 * math.pi / seq) * pos * band
    feats = jnp.where(lane == 0, t, jnp.where(lane <= HY_BANDS, jnp.cos(ang),
                                              jnp.where(lane <= 2 * HY_BANDS, -jnp.sin(ang), 0.0)))
    fr = fr_ref[...]
    h = jnp.sin(fr[0:1] * (jnp.dot(feats, w1_ref[...], precision=HI, preferred_element_type=F32) + b1_ref[...]))
    h = jnp.sin(fr[1:2] * (jnp.dot(h, w2_ref[...], precision=HI, preferred_element_type=F32) + b2_ref[...]))
    h = jnp.dot(h, w3_ref[...], precision=HI, preferred_element_type=F32)
    col = pl.program_id(0) * tn + lax.broadcasted_iota(jnp.int32, (1, tn), 1)
    chan = col % HY_W
    lo, hi = math.log(HY_DECAY_TARGET) / HY_SLOW_DECAY, math.log(HY_DECAY_TARGET) / HY_FAST_DECAY
    delta = jnp.abs(lo + chan.astype(F32) * ((hi - lo) / (HY_W - 1)))
    filt = h * jnp.exp(-t[:, 0:1] * delta)
    is_bwd = (col // HY_W) % 2 == 1
    o_ref[...] = jnp.where(is_bwd & (pos[:, 0:1] == 0.0), 0.0, filt)


def hyena_filters_call(seq, w1, b1, freq, w2, b2, w3, tn=512):
    nf = w1.shape[1]
    n = w3.shape[1]
    w1p = jnp.zeros((HY_FEAT_PAD, nf), F32).at[:w1.shape[0]].set(w1)
    full = lambda shape: pl.BlockSpec(shape, lambda j: (0, 0))
    return pl.pallas_call(
        functools.partial(_hyfilt_kernel, seq=seq),
        grid=(n // tn,),
        in_specs=[full((HY_FEAT_PAD, nf)), full((1, nf)), full((2, nf)), full((nf, nf)), full((1, nf)),
                  pl.BlockSpec((nf, tn), lambda j: (0, j))],
        out_specs=pl.BlockSpec((seq, tn), lambda j: (0, j)),
        out_shape=jax.ShapeDtypeStruct((seq, n), F32),
        compiler_params=_cparams("parallel", vmem_mib=32),
        name="hyena_filters",
    )(w1p, b1.reshape(1, nf), freq, w2, b2.reshape(1, nf), w3)


def _dft_trig(f, s, n, minus_sin):
    k = (f * s + jnp.where(minus_sin, n // 4, 0)) % n
    return jnp.cos(k.astype(F32) * (2.0 * math.pi / n))


def _spectrum_slot(idx, nfreq):
    half = nfreq // 2
    within = idx % nfreq
    return (idx // nfreq) * half + within % half, within >= half


def _dft_fwd_kernel(o_ref, *, seq, nfreq):
    tr = o_ref.shape[0]
    r = pl.program_id(0) * tr + lax.broadcasted_iota(jnp.int32, (tr, seq), 0)
    s = lax.broadcasted_iota(jnp.int32, (tr, seq), 1)
    f, is_im = _spectrum_slot(r, nfreq)
    o_ref[...] = jnp.where(f <= seq, _dft_trig(f, s, 2 * seq, is_im), 0.0).astype(o_ref.dtype)


def _dft_inv_kernel(o_ref, *, seq, nfreq):
    tc = o_ref.shape[1]
    c = pl.program_id(0) * tc + lax.broadcasted_iota(jnp.int32, (seq, tc), 1)
    t = lax.broadcasted_iota(jnp.int32, (seq, tc), 0)
    f, is_im = _spectrum_slot(c, nfreq)
    wf = jnp.where((f == 0) | (f == seq), 1.0, jnp.where(f < seq, 2.0, 0.0)) * (0.5 / seq)
    o_ref[...] = (wf * _dft_trig(f, t, 2 * seq, is_im)).astype(o_ref.dtype)


def dft_tables(seq):
    nfreq = seq + DFT_FREQ_PAD
    fwd = pl.pallas_call(
        functools.partial(_dft_fwd_kernel, seq=seq, nfreq=nfreq), grid=(2 * nfreq // 128,),
        out_specs=pl.BlockSpec((128, seq), lambda i: (i, 0)),
        out_shape=jax.ShapeDtypeStruct((2 * nfreq, seq), BF16), compiler_params=_cparams("parallel", vmem_mib=32),
        name="dft_fwd_table")()
    inv = pl.pallas_call(
        functools.partial(_dft_inv_kernel, seq=seq, nfreq=nfreq), grid=(2 * nfreq // 128,),
        out_specs=pl.BlockSpec((seq, 128), lambda j: (0, j)),
        out_shape=jax.ShapeDtypeStruct((seq, 2 * nfreq), BF16), compiler_params=_cparams("parallel", vmem_mib=32),
        name="dft_inv_table")()
    return fwd, inv


def _dft_apply_kernel(a_ref, x_ref, o_ref, xb_scr):
    @pl.when(pl.program_id(2) == 0)
    def _():
        xb_scr[...] = x_ref[...].astype(BF16)

    o_ref[...] = _dot(a_ref[...], xb_scr[...])


def _dft_mul_kernel(a_ref, x_ref, hf_ref, hb_ref, o_ref, xb_scr):
    @pl.when(pl.program_id(2) == 0)
    def _():
        xb_scr[...] = x_ref[...].astype(BF16)

    u = _dot(a_ref[...], xb_scr[...])
    half = u.shape[0] // 2
    ure, uim = u[:half], u[half:]
    hre = hf_ref[0, :half] + hb_ref[0, :half]
    him = hf_ref[0, half:] - hb_ref[0, half:]
    o_ref[:half] = (ure * hre - uim * him).astype(o_ref.dtype)
    o_ref[half:] = (ure * him + uim * hre).astype(o_ref.dtype)


def dft_apply(table, x, n_samples, col0, width, hspec=None, order=0, tn=512):
    m, ls = table.shape
    tm = m // 2
    if hspec is not None:
        tn = 256
    tn = min(tn, width)
    in_specs = [pl.BlockSpec((tm, ls), lambda s, j, i: (i, 0)),
                pl.BlockSpec((ls, tn), lambda s, j, i: (s, col0 // tn + j))]
    args = (table, x)
    if hspec is not None:
        hcol = order * 2 * width // tn
        in_specs += [pl.BlockSpec((1, tm, tn), lambda s, j, i: (0, i, hcol + j)),
                     pl.BlockSpec((1, tm, tn), lambda s, j, i: (0, i, hcol + width // tn + j))]
        args += (hspec, hspec)
    return pl.pallas_call(
        _dft_apply_kernel if hspec is None else _dft_mul_kernel,
        grid=(n_samples, width // tn, m // tm),
        in_specs=in_specs,
        out_specs=pl.BlockSpec((None, tm, tn), lambda s, j, i: (s, i, j)),
        out_shape=jax.ShapeDtypeStruct((n_samples, m, width), F32 if hspec is None else BF16),
        scratch_shapes=[pltpu.VMEM((ls, tn), BF16)],
        compiler_params=_cparams("parallel", "parallel", "arbitrary", vmem_mib=48),
        name="dft_apply" if hspec is None else "dft_mul",
    )(*args)


def _conv_back_kernel(a_ref, y_ref, gate_ref, prev_ref, skip_ref, o_ref):
    y = _dot(a_ref[...], y_ref[...])
    o_ref[...] = (gate_ref[...] * (y + skip_ref[...] * prev_ref[...])).astype(o_ref.dtype)


def conv_back(inv_table, y, gate, gate_col, prev, prev_col, skip, out_dtype, tm=1024, tn=512):
    ns, m2, c = y.shape
    ls = inv_table.shape[0]
    tm = min(tm, ls)
    return pl.pallas_call(
        _conv_back_kernel,
        grid=(ns, c // tn, ls // tm),
        in_specs=[pl.BlockSpec((tm, m2), lambda s, j, i: (i, 0)),
                  pl.BlockSpec((None, m2, tn), lambda s, j, i: (s, 0, j)),
                  pl.BlockSpec((tm, tn), lambda s, j, i: (s * (ls // tm) + i, gate_col // tn + j)),
                  pl.BlockSpec((tm, tn), lambda s, j, i: (s * (ls // tm) + i, prev_col // tn + j)),
                  pl.BlockSpec((1, tn), lambda s, j, i: (0, j))],
        out_specs=pl.BlockSpec((tm, tn), lambda s, j, i: (s * (ls // tm) + i, j)),
        out_shape=jax.ShapeDtypeStruct((ns * ls, c), out_dtype),
        compiler_params=_cparams("parallel", "parallel", "parallel", vmem_mib=48),
        name="conv_back",
    )(inv_table, y, gate, prev, skip)


def hyena_mixer(p, n_samples, conv_w, conv_b, filt_params, skip):
    ls = p.shape[0] // n_samples
    c = HY_W
    fwd, inv = dft_tables(ls)
    filt = hyena_filters_call(ls, *filt_params)
    hspec = dft_apply(fwd, filt, 1, 0, filt.shape[1])
    xc = conv_silu(p, n_samples, 3 * c, conv_w, conv_b, jnp.ones((3 * c,), F32), silu=False)
    v, v_col = xc, 2 * c
    for order in range(HY_ORDER):
        y = dft_apply(fwd, v, n_samples, v_col, c, hspec=hspec, order=order)
        last = order == HY_ORDER - 1
        v = conv_back(inv, y, xc, order * c, v, v_col, skip[order].reshape(1, c), BF16 if last else F32)
        v_col = 0
    return v


def _pad_cols(w, mult=128):
    n = w.shape[-1]
    return jnp.pad(w, ((0, 0), (0, (-n) % mult)))


def kernel(x, c, ctx, c_ctx, ada_w, ada_b, norm1_g, norm2_g, out_w, router_w, moe_w1, moe_w3, moe_w2, final_g, ev_in_w, hy_conv_w, hy_conv_b, hy_pos_w1, hy_pos_b1, hy_sin_freq, hy_pos_w2, hy_pos_b2, hy_pos_w3, hy_bias, gla_gate_w2, gla_gate_b, gla_norm_g, od_in_w, ml_conv_w, ml_conv_b, ml_igate_b, ml_fgate_b, ml_norm_g, da_lambda, da_norm_g):
    B, L, D = x.shape
    Lc = ctx.shape[1]
    depth = ada_w.shape[0]
    cvec = jnp.zeros((N_MOD_ROWS, D), F32).at[:B].set(c).at[B].set(c_ctx)
    mods = ada_mod(cvec, ada_w, ada_b)
    lat_row = lambda s: s
    ctx_row = lambda s: s * 0 + B
    h_x = x.reshape(B * L, D)
    h_c = ctx.reshape(B * Lc, D)
    tm_lat = min(1024, L)
    for l in range(depth):
        last = l == depth - 1
        mod = mods[l].reshape(N_MOD_ROWS, 1, 6 * D)
        g1 = norm1_g[l].reshape(1, D)
        if l % 2 == 0:
            in_w = ev_in_w[l // 2]
        else:
            od = od_in_w[l // 2]
            ml_w = 4 * HALF_W
            in_w = jnp.concatenate([od[:, :ml_w], od[:, ML_IN:], od[:, ml_w:ML_IN]], axis=1)
        w_in = _pad_cols(in_w, 512).astype(BF16)
        w_out = out_w[l].astype(BF16)
        p_lat2 = in_proj(h_x, g1, mod, lat_row, w_in, tm=L)
        p_ctx2 = in_proj(h_c, g1, mod, ctx_row, w_in, tm=B * Lc)
        if l % 2 == 0:
            e = l // 2
            hy_n = 3 * HY_W
            filt_params = (hy_pos_w1[e], hy_pos_b1[e], hy_sin_freq[e], hy_pos_w2[e], hy_pos_b2[e], hy_pos_w3[e])
            ya_l = hyena_mixer(p_lat2, B, hy_conv_w[e], hy_conv_b[e], filt_params, hy_bias[e])
            ya_c = hyena_mixer(p_ctx2, B, hy_conv_w[e], hy_conv_b[e], filt_params, hy_bias[e])
            yb_c, yb_l = gla_mixer(p_ctx2, p_lat2, gla_gate_w2[e], gla_gate_b[e], gla_norm_g[e], B, hy_n)
        else:
            o = l // 2
            lam_init = 0.8 - 0.6 * math.exp(-0.3 * l)
            ya_l = mlstm_mixer(p_ctx2, p_lat2, ml_conv_w[o], ml_conv_b[o], ml_igate_b[o], ml_fgate_b[o], ml_norm_g[o],
                               B, ml_w + 3 * HALF_W)
            yb_l = diff_attention_call(p_ctx2, p_lat2, B, ml_w, da_lambda[o], da_norm_g[o], lam_init)
        h_x = out_proj(ya_l, yb_l, w_out, h_x, mod, tile_rows(lat_row, L, tm_lat), tm=tm_lat)
        streams = [(h_x, lat_row, B)]
        if not last:
            h_c = out_proj(ya_c, yb_c, w_out, h_c, mod, tile_rows(ctx_row, Lc, Lc), tm=Lc)
            streams.append((h_c, ctx_row, B))
        new = moe_layer(streams, norm2_g[l].reshape(1, D), mod, router_w[l], moe_w1, moe_w3, moe_w2, l)
        h_x = new[0]
        if not last:
            h_c = new[1]
    return final_norm(h_x, final_g.reshape(1, D)).reshape(B, L, D)
```
